```python
import jax, jax.numpy as jnp
from jax import lax
import numpy as np

D_MODEL = 4096
BATCH = 1
SEQ = 8192
DEPTH = 1

GRID_W = 64
CTX_LEN = 256
CHUNK = 128
HEAD_DIM = 128
MIX_WIDTH = D_MODEL
A_WIDTH = MIX_WIDTH // 2
A_GROUPS = A_WIDTH // HEAD_DIM
NA_WIDTH = MIX_WIDTH - A_WIDTH
NA_HEADS = NA_WIDTH // HEAD_DIM
NA_KH = 8
NA_KW = 16
IN_COLS = 2 * A_WIDTH + 3 * NA_WIDTH
KV_START = 2 * A_WIDTH + NA_WIDTH
D_FF = 256 * ((8 * D_MODEL // 3 + 255) // 256)
CONV_W = 3
EPS = 1e-6

kernel_name = "hybrid_gmlp_natten_dit_block"


def rmsnorm(x, g):
    x32 = x.astype(jnp.float32)
    y = x32 * lax.rsqrt(jnp.mean(x32 * x32, axis=-1, keepdims=True) + EPS)
    return (y * g.astype(jnp.float32)).astype(x.dtype)


def layernorm(x, g, b):
    x32 = x.astype(jnp.float32)
    mu = jnp.mean(x32, axis=-1, keepdims=True)
    var = jnp.mean(jnp.square(x32 - mu), axis=-1, keepdims=True)
    y = (x32 - mu) * lax.rsqrt(var + EPS)
    return (y * g.astype(jnp.float32) + b.astype(jnp.float32)).astype(x.dtype)


def adaln_params(cvec, w, b):
    m = jax.nn.silu(cvec) @ w + b
    return jnp.split(m[:, None, :], 6, axis=-1)


def modulate(h, shift, scale):
    return h * (1 + scale) + shift


def chunk_gmlp(u, v, ln_g, ln_b, w_s, b_s):
    B, N, _ = u.shape
    u = jax.nn.gelu(u)
    v = layernorm(jax.nn.gelu(v), ln_g, ln_b)
    v = v.reshape(B, N // CHUNK, CHUNK, A_GROUPS, HEAD_DIM)
    s = jnp.einsum('gpq,bnqgd->bnpgd', w_s, v) + b_s.T[None, None, :, :, None]
    return u * s.reshape(B, N, A_WIDTH)


def neighbourhood_attention(q, k, v, k_ctx, v_ctx, rpb):
    B, N, H, Dh = q.shape
    rows = N // GRID_W
    kh = min(NA_KH, rows)
    scale = Dh ** -0.5
    r = jnp.arange(rows)
    r0 = jnp.clip(r - kh // 2, 0, rows - kh)
    band = r0[:, None] + jnp.arange(kh)
    qg = q.reshape(B, rows, GRID_W, H, Dh)
    kg = k.reshape(B, rows, GRID_W, H, Dh)[:, band]
    vg = v.reshape(B, rows, GRID_W, H, Dh)[:, band]
    col = jnp.arange(GRID_W)
    c0 = jnp.clip(col - NA_KW // 2, 0, GRID_W - NA_KW)
    col_in = (col[None, :] >= c0[:, None]) & (col[None, :] < c0[:, None] + NA_KW)
    dr_idx = band - r[:, None] + (NA_KH - 1)
    dc_idx = jnp.clip(col[None, :] - col[:, None], -(NA_KW - 1), NA_KW - 1) + (NA_KW - 1)
    bias = rpb[:, dr_idx[:, None, :, None], dc_idx[None, :, None, :]]
    s_loc = jnp.einsum('brqhd,brikhd->bhrqik', qg, kg).astype(jnp.float32) * scale
    s_loc = s_loc + bias[None].astype(jnp.float32)
    s_loc = jnp.where(col_in[:, None, :], s_loc, -jnp.inf)
    s_ctx = jnp.einsum('brqhd,bchd->bhrqc', qg, k_ctx).astype(jnp.float32) * scale
    n_loc = kh * GRID_W
    s = jnp.concatenate([s_loc.reshape(B, H, rows, GRID_W, n_loc), s_ctx], axis=-1)
    p = jax.nn.softmax(s, axis=-1).astype(v.dtype)
    p_loc = p[..., :n_loc].reshape(B, H, rows, GRID_W, kh, GRID_W)
    p_ctx = p[..., n_loc:]
    out = jnp.einsum('bhrqik,brikhd->brqhd', p_loc, vg) + jnp.einsum('bhrqc,bchd->brqhd', p_ctx, v_ctx)
    return out.reshape(B, N, H * Dh)


def context_attention(q, k, v):
    s = jnp.einsum('bqhd,bkhd->bhqk', q, k).astype(jnp.float32) * (q.shape[-1] ** -0.5)
    p = jax.nn.softmax(s, axis=-1).astype(v.dtype)
    B, C = q.shape[0], q.shape[1]
    return jnp.einsum('bhqk,bkhd->bqhd', p, v).reshape(B, C, -1)


def conv_ffn(h, w_up, conv_w, conv_b, w_down):
    N = h.shape[1]
    a, g = jnp.split(h @ w_up, 2, axis=-1)
    pad = CONV_W // 2
    ap = jnp.pad(a, ((0, 0), (pad, pad), (0, 0)))
    a = conv_b + sum(ap[:, t:t + N] * conv_w[t] for t in range(CONV_W))
    return (jax.nn.silu(a) * g) @ w_down


def setup_inputs(seed: int = 0) -> dict:
    key = jax.random.key(seed)
    ks = jax.random.split(key, 22)
    f32 = jnp.float32

    def nrm(k, shape, s):
        return jax.random.normal(k, shape, f32) * s

    return {
        "x": nrm(ks[0], (BATCH, SEQ, D_MODEL), 1.0),
        "c": nrm(ks[1], (BATCH, D_MODEL), 1.0),
        "ctx": nrm(ks[2], (BATCH, CTX_LEN, D_MODEL), 1.0),
        "c_ctx": nrm(ks[3], (D_MODEL,), 1.0),
        "w_ada": nrm(ks[4], (DEPTH, D_MODEL, 6 * D_MODEL), D_MODEL ** -0.5),
        "b_ada": nrm(ks[5], (DEPTH, 6 * D_MODEL), 0.02),
        "g_norm1": 1.0 + nrm(ks[6], (DEPTH, D_MODEL), 0.02),
        "w_in": nrm(ks[7], (DEPTH, D_MODEL, IN_COLS), D_MODEL ** -0.5),
        "a_ln_g": 1.0 + nrm(ks[8], (DEPTH, A_WIDTH), 0.02),
        "a_ln_b": nrm(ks[9], (DEPTH, A_WIDTH), 0.02),
        "a_w_s": nrm(ks[10], (DEPTH, A_GROUPS, CHUNK, CHUNK), CHUNK ** -0.5),
        "a_b_s": 1.0 + nrm(ks[11], (DEPTH, A_GROUPS, CHUNK), 0.02),
        "na_rpb": nrm(ks[12], (DEPTH, NA_HEADS, 2 * NA_KH - 1, 2 * NA_KW - 1), 0.1),
        "w_out": nrm(ks[13], (DEPTH, MIX_WIDTH, D_MODEL), MIX_WIDTH ** -0.5),
        "g_norm2": 1.0 + nrm(ks[14], (DEPTH, D_MODEL), 0.02),
        "w_up": nrm(ks[15], (DEPTH, D_MODEL, 2 * D_FF), D_MODEL ** -0.5),
        "conv_w": nrm(ks[16], (DEPTH, CONV_W, D_FF), CONV_W ** -0.5),
        "conv_b": nrm(ks[17], (DEPTH, D_FF), 0.02),
        "w_down": nrm(ks[18], (DEPTH, D_FF, D_MODEL), D_FF ** -0.5),
        "g_final": 1.0 + nrm(ks[19], (D_MODEL,), 0.02),
    }


def reference(x, c, ctx, c_ctx, w_ada, b_ada, g_norm1, w_in, a_ln_g, a_ln_b, a_w_s, a_b_s,
              na_rpb, w_out, g_norm2, w_up, conv_w, conv_b, w_down, g_final):
    B, N, _ = x.shape
    for l in range(DEPTH):
        sh1, sc1, gt1, sh2, sc2, gt2 = adaln_params(c, w_ada[l], b_ada[l])
        csh1, csc1, cgt1, csh2, csc2, cgt2 = adaln_params(c_ctx[None], w_ada[l], b_ada[l])

        h = modulate(rmsnorm(x, g_norm1[l]), sh1, sc1)
        hc = modulate(rmsnorm(ctx, g_norm1[l]), csh1, csc1)
        u, v, q, k, va = jnp.split(h @ w_in[l], [A_WIDTH, 2 * A_WIDTH, 2 * A_WIDTH + NA_WIDTH,
                                                KV_START + NA_WIDTH], axis=-1)
        kc, vc = jnp.split(hc @ w_in[l][:, KV_START:], 2, axis=-1)
        C = ctx.shape[1]
        kc = kc.reshape(B, C, NA_HEADS, HEAD_DIM)
        vc = vc.reshape(B, C, NA_HEADS, HEAD_DIM)
        y_a = chunk_gmlp(u, v, a_ln_g[l], a_ln_b[l], a_w_s[l], a_b_s[l])
        y_b = neighbourhood_attention(q.reshape(B, N, NA_HEADS, HEAD_DIM),
                                      k.reshape(B, N, NA_HEADS, HEAD_DIM),
                                      va.reshape(B, N, NA_HEADS, HEAD_DIM), kc, vc, na_rpb[l])
        x_new = x + gt1 * (jnp.concatenate([y_a, y_b], axis=-1) @ w_out[l])

        h2 = modulate(rmsnorm(x_new, g_norm2[l]), sh2, sc2)
        x_new = x_new + gt2 * conv_ffn(h2, w_up[l], conv_w[l], conv_b[l], w_down[l])

        if l + 1 < DEPTH:
            uc, vgc, qc = jnp.split(hc @ w_in[l][:, :KV_START], [A_WIDTH, 2 * A_WIDTH], axis=-1)
            yc_a = chunk_gmlp(uc, vgc, a_ln_g[l], a_ln_b[l], a_w_s[l], a_b_s[l])
            yc_b = context_attention(qc.reshape(B, C, NA_HEADS, HEAD_DIM), kc, vc)
            ctx = ctx + cgt1 * (jnp.concatenate([yc_a, yc_b], axis=-1) @ w_out[l])
            hc2 = modulate(rmsnorm(ctx, g_norm2[l]), csh2, csc2)
            ctx = ctx + cgt2 * conv_ffn(hc2, w_up[l], conv_w[l], conv_b[l], w_down[l])
        x = x_new
    return rmsnorm(x, g_final)
```

```python
import functools
import math

import jax
import jax.numpy as jnp
from jax import lax
from jax.experimental import pallas as pl
from jax.experimental.pallas import tpu as pltpu

F32 = jnp.float32
BF16 = jnp.bfloat16

D_MODEL = 4096
SEQ = 8192
GRID_W = 64
ROWS = SEQ // GRID_W
CTX_LEN = 256
CHUNK = 128
HEAD_DIM = 128
A_WIDTH = 2048
A_GROUPS = A_WIDTH // HEAD_DIM
NA_WIDTH = 2048
NA_HEADS = NA_WIDTH // HEAD_DIM
NA_KH = 8
NA_KW = 16
IN_COLS = 2 * A_WIDTH + 3 * NA_WIDTH
KV_START = 2 * A_WIDTH + NA_WIDTH
D_FF = 11008
D_FF_PAD = 11264
EPS = 1e-6

BF16_SUBLANES = 16
VMEM_LIMIT = 56 * 1024 * 1024


def _params(*sem):
    return pltpu.CompilerParams(dimension_semantics=sem, vmem_limit_bytes=VMEM_LIMIT)


def _ada_kernel(c_ref, w_ref, b_ref, o_ref):
    c = c_ref[...]
    a = (c * jax.nn.sigmoid(c)).astype(BF16)
    w = w_ref[...].astype(BF16)
    o_ref[...] = jnp.dot(a, w, preferred_element_type=F32) + b_ref[...]


def _adaln(cs, w, b, bn=512):
    rows, d = cs.shape
    n = w.shape[1]
    return pl.pallas_call(
        _ada_kernel,
        grid=(n // bn,),
        in_specs=[
            pl.BlockSpec((rows, d), lambda j: (0, 0)),
            pl.BlockSpec((d, bn), lambda j: (0, j)),
            pl.BlockSpec((1, bn), lambda j: (0, j)),
        ],
        out_specs=pl.BlockSpec((rows, bn), lambda j: (0, j)),
        out_shape=jax.ShapeDtypeStruct((rows, n), F32),
        compiler_params=_params("arbitrary"),
        name="adaln",
    )(cs, w, b)


def _norm_mod_kernel(x_ref, g_ref, sc_ref, sh_ref, o_ref, *, pad, nblk):
    def body():
        x = x_ref[...]
        y = x * lax.rsqrt(jnp.mean(x * x, axis=-1, keepdims=True) + EPS)
        y = y * g_ref[...]
        o_ref[...] = (y * (1.0 + sc_ref[...]) + sh_ref[...]).astype(o_ref.dtype)

    if pad:
        i = pl.program_id(0)
        is_pad = (i == 0) | (i == nblk + 1)

        @pl.when(is_pad)
        def _():
            o_ref[...] = jnp.zeros(o_ref.shape, o_ref.dtype)

        pl.when(jnp.logical_not(is_pad))(body)
    else:
        body()


def _norm_mod(x, g, sc, sh, bm=256, pad=False):
    m, d = x.shape
    nblk = m // bm
    vec = pl.BlockSpec((1, d), lambda i: (0, 0))
    if pad:
        x_spec = pl.BlockSpec((bm, d), lambda i: (jnp.clip(i - 1, 0, nblk - 1), 0))
        steps, rows = nblk + 2, m + 2 * bm
    else:
        x_spec = pl.BlockSpec((bm, d), lambda i: (i, 0))
        steps, rows = nblk, m
    return pl.pallas_call(
        functools.partial(_norm_mod_kernel, pad=pad, nblk=nblk),
        grid=(steps,),
        in_specs=[x_spec, vec, vec, vec],
        out_specs=pl.BlockSpec((bm, d), lambda i: (i, 0)),
        out_shape=jax.ShapeDtypeStruct((rows, d), BF16),
        compiler_params=_params("arbitrary"),
        name="norm_mod",
    )(x, g, sc, sh)


def _norm_kernel(x_ref, g_ref, o_ref):
    x = x_ref[...]
    y = x * lax.rsqrt(jnp.mean(x * x, axis=-1, keepdims=True) + EPS)
    o_ref[...] = y * g_ref[...]


def _final_norm(x, g, bm=256):
    m, d = x.shape
    return pl.pallas_call(
        _norm_kernel,
        grid=(m // bm,),
        in_specs=[pl.BlockSpec((bm, d), lambda i: (i, 0)), pl.BlockSpec((1, d), lambda i: (0, 0))],
        out_specs=pl.BlockSpec((bm, d), lambda i: (i, 0)),
        out_shape=jax.ShapeDtypeStruct((m, d), F32),
        compiler_params=_params("arbitrary"),
        name="final_norm",
    )(x, g)


def _proj_kernel(a_ref, w_ref, o_ref):
    o_ref[...] = jnp.dot(a_ref[...], w_ref[...], preferred_element_type=F32).astype(o_ref.dtype)


def _proj(a, w, col0, n, bm, bn):
    m, k = a.shape
    off = col0 // bn
    return pl.pallas_call(
        _proj_kernel,
        grid=(m // bm, n // bn),
        in_specs=[
            pl.BlockSpec((bm, k), lambda i, j: (i, 0)),
            pl.BlockSpec((k, bn), lambda i, j: (0, j + off)),
        ],
        out_specs=pl.BlockSpec((bm, bn), lambda i, j: (i, j)),
        out_shape=jax.ShapeDtypeStruct((m, n), BF16),
        compiler_params=_params("arbitrary", "arbitrary"),
        name="proj",
    )(a, w)


def _gelu(x):
    return x * (0.5 * (1.0 + jnp.tanh(math.sqrt(2.0 / math.pi) * (x + 0.044715 * (x * x * x)))))


def _gmlp_kernel(h_ref, wu_ref, wv_ref, lng_ref, lnb_ref, ws_ref, bs_ref, o_ref,
                 u_scr, v_scr, s1_scr, s2_scr, *, nj, bn, bm):
    j = pl.program_id(1)
    h = h_ref[...]
    gu = _gelu(jnp.dot(h, wu_ref[...], preferred_element_type=F32))
    gv = _gelu(jnp.dot(h, wv_ref[...], preferred_element_type=F32))
    col = pl.multiple_of(j * bn, bn)
    u_scr[:, pl.ds(col, bn)] = gu
    v_scr[:, pl.ds(col, bn)] = gv
    ps1 = jnp.sum(gv, axis=-1, keepdims=True)
    ps2 = jnp.sum(gv * gv, axis=-1, keepdims=True)

    @pl.when(j == 0)
    def _():
        s1_scr[...] = ps1
        s2_scr[...] = ps2

    @pl.when(j > 0)
    def _():
        s1_scr[...] += ps1
        s2_scr[...] += ps2

    @pl.when(j == nj - 1)
    def _():
        mu = s1_scr[...] * (1.0 / A_WIDTH)
        var = s2_scr[...] * (1.0 / A_WIDTH) - mu * mu
        rstd = lax.rsqrt(var + EPS)
        nchunk = bm // CHUNK
        for g in range(A_GROUPS):
            cs = slice(g * HEAD_DIM, (g + 1) * HEAD_DIM)
            vn = (v_scr[:, cs] - mu) * rstd * lng_ref[:, cs] + lnb_ref[:, cs]
            vn = vn.astype(BF16)
            vcat = jnp.concatenate([vn[c * CHUNK:(c + 1) * CHUNK, :] for c in range(nchunk)], axis=1)
            s = jnp.dot(ws_ref[g], vcat, preferred_element_type=F32) + bs_ref[g]
            for c in range(nchunk):
                rs = slice(c * CHUNK, (c + 1) * CHUNK)
                o_ref[rs, cs] = (u_scr[rs, cs] * s[:, c * HEAD_DIM:(c + 1) * HEAD_DIM]).astype(o_ref.dtype)


def _gmlp(h, w_in, ln_g, ln_b, w_s, b_s, bm=512, bn=512):
    m, k = h.shape
    nj = A_WIDTH // bn
    kern = functools.partial(_gmlp_kernel, nj=nj, bn=bn, bm=bm)
    return pl.pallas_call(
        kern,
        grid=(m // bm, nj),
        in_specs=[
            pl.BlockSpec((bm, k), lambda i, j: (i, 0)),
            pl.BlockSpec((k, bn), lambda i, j: (0, j)),
            pl.BlockSpec((k, bn), lambda i, j: (0, j + nj)),
            pl.BlockSpec((1, A_WIDTH), lambda i, j: (0, 0)),
            pl.BlockSpec((1, A_WIDTH), lambda i, j: (0, 0)),
            pl.BlockSpec((A_GROUPS, CHUNK, CHUNK), lambda i, j: (0, 0, 0)),
            pl.BlockSpec((A_GROUPS, CHUNK, 1), lambda i, j: (0, 0, 0)),
        ],
        out_specs=pl.BlockSpec((bm, A_WIDTH), lambda i, j: (i, 0)),
        out_shape=jax.ShapeDtypeStruct((m, A_WIDTH), BF16),
        scratch_shapes=[
            pltpu.VMEM((bm, A_WIDTH), F32),
            pltpu.VMEM((bm, A_WIDTH), F32),
            pltpu.VMEM((bm, 1), F32),
            pltpu.VMEM((bm, 1), F32),
        ],
        compiler_params=_params("arbitrary", "arbitrary"),
        name="gmlp",
    )(h, w_in, w_in, ln_g, ln_b, w_s, b_s)


N_DR = 2 * NA_KH - 1
N_DC = 2 * NA_KW - 1
N_PAIR = N_DR - 1


def _bias_kernel(rpb_ref, o_ref):
    h = pl.program_id(0)
    qc = lax.broadcasted_iota(jnp.int32, (GRID_W, 2 * GRID_W), 0)
    lane = lax.broadcasted_iota(jnp.int32, (GRID_W, 2 * GRID_W), 1)
    kc = lane & (GRID_W - 1)
    hi = lane >= GRID_W
    dc = jnp.clip(kc - qc, -(NA_KW - 1), NA_KW - 1) + (NA_KW - 1)
    c0 = jnp.clip(qc - NA_KW // 2, 0, GRID_W - NA_KW)
    col_in = (kc >= c0) & (kc < c0 + NA_KW)
    hit = [dc == t for t in range(N_DC)]
    for d in range(N_PAIR):
        acc = jnp.zeros((GRID_W, 2 * GRID_W), F32)
        for t in range(N_DC):
            val = jnp.where(hi, rpb_ref[h, (d + 1) * N_DC + t], rpb_ref[h, d * N_DC + t])
            acc = jnp.where(hit[t], val, acc)
        o_ref[0, d] = jnp.where(col_in, acc, -jnp.inf)


def _bias_table(rpb):
    rpb2 = rpb.reshape(NA_HEADS, N_DR * N_DC)
    return pl.pallas_call(
        _bias_kernel,
        grid=(NA_HEADS,),
        in_specs=[pl.BlockSpec(memory_space=pltpu.SMEM)],
        out_specs=pl.BlockSpec((1, N_PAIR, GRID_W, 2 * GRID_W), lambda h: (h, 0, 0, 0)),
        out_shape=jax.ShapeDtypeStruct((NA_HEADS, N_PAIR, GRID_W, 2 * GRID_W), F32),
        compiler_params=_params("arbitrary"),
        name="bias_table",
    )(rpb2)


_NT = (((1,), (1,)), ((), ()))


def _attn_kernel(q_ref, k_ref, v_ref, kc_ref, vc_ref, tab_ref, o_ref, *, rb):
    blk = pl.program_id(1)
    scale = HEAD_DIM ** -0.5
    kc = kc_ref[...]
    vc = vc_ref[...]
    n_loc = NA_KH * GRID_W
    for a in range(rb):
        r = blk * rb + a
        r0 = jnp.clip(r - NA_KH // 2, 0, ROWS - NA_KH)
        shift = r0 - r + (NA_KH - 1)
        start = pl.multiple_of(r0 * GRID_W, GRID_W)
        q = q_ref[a * GRID_W:(a + 1) * GRID_W, :]
        kw = k_ref[pl.ds(start, n_loc), :]
        vw = v_ref[pl.ds(start, n_loc), :]
        bias = jnp.concatenate([tab_ref[0, shift + 2 * t] for t in range(NA_KH // 2)], axis=1)
        s_loc = lax.dot_general(q, kw, _NT, preferred_element_type=F32) * scale + bias
        s_ctx = lax.dot_general(q, kc, _NT, preferred_element_type=F32) * scale
        m = jnp.maximum(jnp.max(s_loc, axis=-1, keepdims=True), jnp.max(s_ctx, axis=-1, keepdims=True))
        p_loc = jnp.exp(s_loc - m)
        p_ctx = jnp.exp(s_ctx - m)
        denom = jnp.sum(p_loc, axis=-1, keepdims=True) + jnp.sum(p_ctx, axis=-1, keepdims=True)
        o = jnp.dot(p_loc.astype(BF16), vw, preferred_element_type=F32)
        o = o + jnp.dot(p_ctx.astype(BF16), vc, preferred_element_type=F32)
        o_ref[a * GRID_W:(a + 1) * GRID_W, :] = (o / denom).astype(o_ref.dtype)


def _attention(qkv, kvc, tab, rb=8):
    n = qkv.shape[0]
    c = kvc.shape[0]
    kern = functools.partial(_attn_kernel, rb=rb)
    return pl.pallas_call(
        kern,
        grid=(NA_HEADS, ROWS // rb),
        in_specs=[
            pl.BlockSpec((rb * GRID_W, HEAD_DIM), lambda h, b: (b, h)),
            pl.BlockSpec((n, HEAD_DIM), lambda h, b: (0, NA_HEADS + h)),
            pl.BlockSpec((n, HEAD_DIM), lambda h, b: (0, 2 * NA_HEADS + h)),
            pl.BlockSpec((c, HEAD_DIM), lambda h, b: (0, h)),
            pl.BlockSpec((c, HEAD_DIM), lambda h, b: (0, NA_HEADS + h)),
            pl.BlockSpec((1, N_PAIR, GRID_W, 2 * GRID_W), lambda h, b: (h, 0, 0, 0)),
        ],
        out_specs=pl.BlockSpec((rb * GRID_W, HEAD_DIM), lambda h, b: (b, h)),
        out_shape=jax.ShapeDtypeStruct((n, NA_WIDTH), BF16),
        compiler_params=_params("arbitrary", "arbitrary"),
        name="attention",
    )(qkv, qkv, qkv, kvc, kvc, tab)


def _outproj_kernel(ya_ref, yb_ref, w_ref, x_ref, gt_ref, o_ref):
    acc = jnp.dot(ya_ref[...], w_ref[:A_WIDTH, :], preferred_element_type=F32)
    acc = acc + jnp.dot(yb_ref[...], w_ref[A_WIDTH:, :], preferred_element_type=F32)
    o_ref[...] = x_ref[...] + gt_ref[...] * acc


def _outproj(ya, yb, w, x, gt, bm=1024, bn=512):
    m = ya.shape[0]
    k, n = w.shape
    return pl.pallas_call(
        _outproj_kernel,
        grid=(m // bm, n // bn),
        in_specs=[
            pl.BlockSpec((bm, A_WIDTH), lambda i, j: (i, 0)),
            pl.BlockSpec((bm, NA_WIDTH), lambda i, j: (i, 0)),
            pl.BlockSpec((k, bn), lambda i, j: (0, j)),
            pl.BlockSpec((bm, bn), lambda i, j: (i, j)),
            pl.BlockSpec((1, bn), lambda i, j: (0, j)),
        ],
        out_specs=pl.BlockSpec((bm, bn), lambda i, j: (i, j)),
        out_shape=jax.ShapeDtypeStruct((m, n), F32),
        compiler_params=_params("arbitrary", "arbitrary"),
        name="outproj",
    )(ya, yb, w, x, gt)


HALO = BF16_SUBLANES


def _upproj_kernel(h_ref, wa_ref, wg_ref, cw_ref, cb_ref, o_ref, a_scr, *, bm):
    a_scr[...] = jnp.dot(h_ref[...], wa_ref[...], preferred_element_type=F32)
    g = jnp.dot(h_ref[HALO:HALO + bm, :], wg_ref[...], preferred_element_type=F32)
    a_prev = a_scr[pl.ds(HALO - 1, bm), :]
    a_mid = a_scr[pl.ds(HALO, bm), :]
    a_next = a_scr[pl.ds(HALO + 1, bm), :]
    a = cb_ref[...] + ((a_prev * cw_ref[0:1, :] + a_mid * cw_ref[1:2, :]) + a_next * cw_ref[2:3, :])
    o_ref[...] = ((a * jax.nn.sigmoid(a)) * g).astype(o_ref.dtype)


def _upproj(h2p, wa, wg, cw, cb, m, pad_rows, bm=1024, bn=512):
    k = h2p.shape[1]
    n = wa.shape[1]
    kern = functools.partial(_upproj_kernel, bm=bm)
    return pl.pallas_call(
        kern,
        grid=(m // bm, n // bn),
        in_specs=[
            pl.BlockSpec((pl.Element(bm + 2 * HALO), pl.Element(k)),
                         lambda i, j: (pl.multiple_of(pad_rows - HALO + i * bm, HALO), 0)),
            pl.BlockSpec((k, bn), lambda i, j: (0, j)),
            pl.BlockSpec((k, bn), lambda i, j: (0, j)),
            pl.BlockSpec((3, bn), lambda i, j: (0, j)),
            pl.BlockSpec((1, bn), lambda i, j: (0, j)),
        ],
        out_specs=pl.BlockSpec((bm, bn), lambda i, j: (i, j)),
        out_shape=jax.ShapeDtypeStruct((m, n), BF16),
        scratch_shapes=[pltpu.VMEM((bm + 2 * HALO, bn), F32)],
        compiler_params=_params("arbitrary", "arbitrary"),
        name="upproj",
    )(h2p, wa, wg, cw, cb)


def _downproj_kernel(a_ref, w_ref, x_ref, gt_ref, o_ref, acc_ref, *, nk):
    kk = pl.program_id(2)
    part = jnp.dot(a_ref[...], w_ref[...], preferred_element_type=F32)

    @pl.when(kk == 0)
    def _():
        acc_ref[...] = part

    @pl.when(kk > 0)
    def _():
        acc_ref[...] += part

    @pl.when(kk == nk - 1)
    def _():
        o_ref[...] = x_ref[...] + gt_ref[...] * acc_ref[...]


def _downproj(a, w, x, gt, bm=1024, bn=1024, bk=2816):
    m, k = a.shape
    n = w.shape[1]
    nk = k // bk
    kern = functools.partial(_downproj_kernel, nk=nk)
    return pl.pallas_call(
        kern,
        grid=(m // bm, n // bn, nk),
        in_specs=[
            pl.BlockSpec((bm, bk), lambda i, j, kk: (i, kk)),
            pl.BlockSpec((bk, bn), lambda i, j, kk: (kk, j)),
            pl.BlockSpec((bm, bn), lambda i, j, kk: (i, j)),
            pl.BlockSpec((1, bn), lambda i, j, kk: (0, j)),
        ],
        out_specs=pl.BlockSpec((bm, bn), lambda i, j, kk: (i, j)),
        out_shape=jax.ShapeDtypeStruct((m, n), F32),
        scratch_shapes=[pltpu.VMEM((bm, bn), F32)],
        compiler_params=_params("arbitrary", "arbitrary", "arbitrary"),
        name="downproj",
    )(a, w, x, gt)


def kernel(x, c, ctx, c_ctx, w_ada, b_ada, g_norm1, w_in, a_ln_g, a_ln_b, a_w_s, a_b_s, na_rpb, w_out,
           g_norm2, w_up, conv_w, conv_b, w_down, g_final):
    d = D_MODEL
    x2 = x[0]
    ctx2 = ctx[0]

    w_in_b = w_in[0].astype(BF16)
    w_out_b = w_out[0].astype(BF16)
    pad_ff = D_FF_PAD - D_FF
    w_up_a = jnp.pad(w_up[0][:, :D_FF].astype(BF16), ((0, 0), (0, pad_ff)))
    w_up_g = jnp.pad(w_up[0][:, D_FF:].astype(BF16), ((0, 0), (0, pad_ff)))
    w_down_b = jnp.pad(w_down[0].astype(BF16), ((0, pad_ff), (0, 0)))
    cw = jnp.pad(conv_w[0], ((0, 0), (0, pad_ff)))
    cb = jnp.pad(conv_b[0], (0, pad_ff)).reshape(1, D_FF_PAD)
    w_s_b = a_w_s[0].astype(BF16)
    b_s = a_b_s[0].reshape(A_GROUPS, CHUNK, 1)

    cs = jnp.zeros((BF16_SUBLANES, d), F32).at[0].set(c[0]).at[1].set(c_ctx)
    mod = _adaln(cs, w_ada[0], b_ada[0].reshape(1, 6 * d))
    sh1, sc1, gt1, sh2, sc2, gt2 = [mod[0:1, t * d:(t + 1) * d] for t in range(6)]
    csh1, csc1 = mod[1:2, 0:d], mod[1:2, d:2 * d]

    g1 = g_norm1[0].reshape(1, d)
    h = _norm_mod(x2, g1, sc1, sh1)
    hc = _norm_mod(ctx2, g1, csc1, csh1)

    y_a = _gmlp(h, w_in_b, a_ln_g[0].reshape(1, A_WIDTH), a_ln_b[0].reshape(1, A_WIDTH), w_s_b, b_s)
    qkv = _proj(h, w_in_b, 2 * A_WIDTH, 3 * NA_WIDTH, bm=1024, bn=1024)
    kvc = _proj(hc, w_in_b, KV_START, 2 * NA_WIDTH, bm=CTX_LEN, bn=1024)
    tab = _bias_table(na_rpb[0])
    y_b = _attention(qkv, kvc, tab)

    x_new = _outproj(y_a, y_b, w_out_b, x2, gt1)
    norm_bm = 256
    h2p = _norm_mod(x_new, g_norm2[0].reshape(1, d), sc2, sh2, bm=norm_bm, pad=True)
    hid = _upproj(h2p, w_up_a, w_up_g, cw, cb, m=SEQ, pad_rows=norm_bm)
    x_fin = _downproj(hid, w_down_b, x_new, gt2)
    out = _final_norm(x_fin, g_final.reshape(1, d))
    return out[None]
```

```python
import functools
import math

import jax
import jax.numpy as jnp
from jax import lax
from jax.experimental import pallas as pl
from jax.experimental.pallas import tpu as pltpu

F32 = jnp.float32
BF16 = jnp.bfloat16

D_MODEL = 4096
SEQ = 8192
GRID_W = 64
ROWS = SEQ // GRID_W
CTX_LEN = 256
CHUNK = 128
HEAD_DIM = 128
A_WIDTH = 2048
A_GROUPS = A_WIDTH // HEAD_DIM
NA_WIDTH = 2048
NA_HEADS = NA_WIDTH // HEAD_DIM
NA_KH = 8
NA_KW = 16
IN_COLS = 2 * A_WIDTH + 3 * NA_WIDTH
KV_START = 2 * A_WIDTH + NA_WIDTH
D_FF = 11008
EPS = 1e-6

BF16_SUBLANES = 16
VMEM_LIMIT = 56 * 1024 * 1024


def _params(*sem):
    return pltpu.CompilerParams(dimension_semantics=sem, vmem_limit_bytes=VMEM_LIMIT)


def _ada_kernel(c_ref, w_ref, b_ref, o_ref):
    c = c_ref[...]
    a = (c * jax.nn.sigmoid(c)).astype(BF16)
    w = w_ref[...].astype(BF16)
    o_ref[...] = jnp.dot(a, w, preferred_element_type=F32) + b_ref[...]


def _adaln(cs, w, b, bn=512):
    rows, d = cs.shape
    n = w.shape[1]
    return pl.pallas_call(
        _ada_kernel,
        grid=(n // bn,),
        in_specs=[
            pl.BlockSpec((rows, d), lambda j: (0, 0)),
            pl.BlockSpec((d, bn), lambda j: (0, j)),
            pl.BlockSpec((1, bn), lambda j: (0, j)),
        ],
        out_specs=pl.BlockSpec((rows, bn), lambda j: (0, j)),
        out_shape=jax.ShapeDtypeStruct((rows, n), F32),
        compiler_params=_params("arbitrary"),
        name="adaln",
    )(cs, w, b)


def _norm_mod_kernel(x_ref, g_ref, sc_ref, sh_ref, o_ref, *, pad, nblk):
    def body():
        x = x_ref[...]
        y = x * lax.rsqrt(jnp.mean(x * x, axis=-1, keepdims=True) + EPS)
        y = y * g_ref[...]
        o_ref[...] = (y * (1.0 + sc_ref[...]) + sh_ref[...]).astype(o_ref.dtype)

    if pad:
        i = pl.program_id(0)
        is_pad = (i == 0) | (i == nblk + 1)

        @pl.when(is_pad)
        def _():
            o_ref[...] = jnp.zeros(o_ref.shape, o_ref.dtype)

        pl.when(jnp.logical_not(is_pad))(body)
    else:
        body()


def _norm_mod(x, g, sc, sh, bm=256, pad=False):
    m, d = x.shape
    nblk = m // bm
    vec = pl.BlockSpec((1, d), lambda i: (0, 0))
    if pad:
        x_spec = pl.BlockSpec((bm, d), lambda i: (jnp.clip(i - 1, 0, nblk - 1), 0))
        steps, rows = nblk + 2, m + 2 * bm
    else:
        x_spec = pl.BlockSpec((bm, d), lambda i: (i, 0))
        steps, rows = nblk, m
    return pl.pallas_call(
        functools.partial(_norm_mod_kernel, pad=pad, nblk=nblk),
        grid=(steps,),
        in_specs=[x_spec, vec, vec, vec],
        out_specs=pl.BlockSpec((bm, d), lambda i: (i, 0)),
        out_shape=jax.ShapeDtypeStruct((rows, d), BF16),
        compiler_params=_params("arbitrary"),
        name="norm_mod",
    )(x, g, sc, sh)


def _norm_kernel(x_ref, g_ref, o_ref):
    x = x_ref[...]
    y = x * lax.rsqrt(jnp.mean(x * x, axis=-1, keepdims=True) + EPS)
    o_ref[...] = y * g_ref[...]


def _final_norm(x, g, bm=256):
    m, d = x.shape
    return pl.pallas_call(
        _norm_kernel,
        grid=(m // bm,),
        in_specs=[pl.BlockSpec((bm, d), lambda i: (i, 0)), pl.BlockSpec((1, d), lambda i: (0, 0))],
        out_specs=pl.BlockSpec((bm, d), lambda i: (i, 0)),
        out_shape=jax.ShapeDtypeStruct((m, d), F32),
        compiler_params=_params("arbitrary"),
        name="final_norm",
    )(x, g)


def _proj_kernel(a_ref, w_ref, o_ref):
    o_ref[...] = jnp.dot(a_ref[...], w_ref[...], preferred_element_type=F32).astype(o_ref.dtype)


def _proj(a, w, col0, n, bm, bn):
    m, k = a.shape
    off = col0 // bn
    return pl.pallas_call(
        _proj_kernel,
        grid=(m // bm, n // bn),
        in_specs=[
            pl.BlockSpec((bm, k), lambda i, j: (i, 0)),
            pl.BlockSpec((k, bn), lambda i, j: (0, j + off)),
        ],
        out_specs=pl.BlockSpec((bm, bn), lambda i, j: (i, j)),
        out_shape=jax.ShapeDtypeStruct((m, n), BF16),
        compiler_params=_params("arbitrary", "arbitrary"),
        name="proj",
    )(a, w)


def _gelu(x):
    return x * (0.5 * (1.0 + jnp.tanh(math.sqrt(2.0 / math.pi) * (x + 0.044715 * (x * x * x)))))


def _gmlp_kernel(h_ref, wu_ref, wv_ref, lng_ref, lnb_ref, ws_ref, bs_ref, o_ref,
                 u_scr, v_scr, s1_scr, s2_scr, *, nj, bn, bm):
    j = pl.program_id(1)
    h = h_ref[...]
    gu = _gelu(jnp.dot(h, wu_ref[...], preferred_element_type=F32))
    gv = _gelu(jnp.dot(h, wv_ref[...], preferred_element_type=F32))
    col = pl.multiple_of(j * bn, bn)
    u_scr[:, pl.ds(col, bn)] = gu
    v_scr[:, pl.ds(col, bn)] = gv
    ps1 = jnp.sum(gv, axis=-1, keepdims=True)
    ps2 = jnp.sum(gv * gv, axis=-1, keepdims=True)

    @pl.when(j == 0)
    def _():
        s1_scr[...] = ps1
        s2_scr[...] = ps2

    @pl.when(j > 0)
    def _():
        s1_scr[...] += ps1
        s2_scr[...] += ps2

    @pl.when(j == nj - 1)
    def _():
        mu = s1_scr[...] * (1.0 / A_WIDTH)
        var = s2_scr[...] * (1.0 / A_WIDTH) - mu * mu
        rstd = lax.rsqrt(var + EPS)
        nchunk = bm // CHUNK
        for g in range(A_GROUPS):
            cs = slice(g * HEAD_DIM, (g + 1) * HEAD_DIM)
            vn = (v_scr[:, cs] - mu) * rstd * lng_ref[:, cs] + lnb_ref[:, cs]
            vn = vn.astype(BF16)
            vcat = jnp.concatenate([vn[c * CHUNK:(c + 1) * CHUNK, :] for c in range(nchunk)], axis=1)
            s = jnp.dot(ws_ref[g], vcat, preferred_element_type=F32) + bs_ref[g]
            for c in range(nchunk):
                rs = slice(c * CHUNK, (c + 1) * CHUNK)
                o_ref[rs, cs] = (u_scr[rs, cs] * s[:, c * HEAD_DIM:(c + 1) * HEAD_DIM]).astype(o_ref.dtype)


def _gmlp(h, w_in, ln_g, ln_b, w_s, b_s, bm=512, bn=512):
    m, k = h.shape
    nj = A_WIDTH // bn
    kern = functools.partial(_gmlp_kernel, nj=nj, bn=bn, bm=bm)
    return pl.pallas_call(
        kern,
        grid=(m // bm, nj),
        in_specs=[
            pl.BlockSpec((bm, k), lambda i, j: (i, 0)),
            pl.BlockSpec((k, bn), lambda i, j: (0, j)),
            pl.BlockSpec((k, bn), lambda i, j: (0, j + nj)),
            pl.BlockSpec((1, A_WIDTH), lambda i, j: (0, 0)),
            pl.BlockSpec((1, A_WIDTH), lambda i, j: (0, 0)),
            pl.BlockSpec((A_GROUPS, CHUNK, CHUNK), lambda i, j: (0, 0, 0)),
            pl.BlockSpec((A_GROUPS, CHUNK, 1), lambda i, j: (0, 0, 0)),
        ],
        out_specs=pl.BlockSpec((bm, A_WIDTH), lambda i, j: (i, 0)),
        out_shape=jax.ShapeDtypeStruct((m, A_WIDTH), BF16),
        scratch_shapes=[
            pltpu.VMEM((bm, A_WIDTH), F32),
            pltpu.VMEM((bm, A_WIDTH), F32),
            pltpu.VMEM((bm, 1), F32),
            pltpu.VMEM((bm, 1), F32),
        ],
        compiler_params=_params("arbitrary", "arbitrary"),
        name="gmlp",
    )(h, w_in, w_in, ln_g, ln_b, w_s, b_s)


N_DR = 2 * NA_KH - 1
N_DC = 2 * NA_KW - 1
TAB_BOTH = 0
TAB_LEFT = N_DR - 1
TAB_RIGHT = 2 * N_DR - 1
N_TAB = 3 * N_DR - 1


def _bias_kernel(rpb_ref, o_ref):
    h = pl.program_id(0)
    qc = lax.broadcasted_iota(jnp.int32, (GRID_W, 2 * GRID_W), 0)
    lane = lax.broadcasted_iota(jnp.int32, (GRID_W, 2 * GRID_W), 1)
    kc = lane & (GRID_W - 1)
    hi = lane >= GRID_W
    dc = jnp.clip(kc - qc, -(NA_KW - 1), NA_KW - 1) + (NA_KW - 1)
    c0 = jnp.clip(qc - NA_KW // 2, 0, GRID_W - NA_KW)
    col_in = (kc >= c0) & (kc < c0 + NA_KW)
    hit = [dc == t for t in range(N_DC)]
    neg = jnp.full((GRID_W, 2 * GRID_W), -jnp.inf, F32)
    b = []
    for d in range(N_DR):
        acc = jnp.zeros((GRID_W, 2 * GRID_W), F32)
        for t in range(N_DC):
            acc = jnp.where(hit[t], rpb_ref[h, d * N_DC + t], acc)
        b.append(jnp.where(col_in, acc, neg))
    for d in range(N_DR):
        if d + 1 < N_DR:
            o_ref[0, TAB_BOTH + d] = jnp.where(hi, b[d + 1], b[d])
        o_ref[0, TAB_LEFT + d] = jnp.where(hi, neg, b[d])
        o_ref[0, TAB_RIGHT + d] = jnp.where(hi, b[d], neg)


def _bias_table(rpb):
    rpb2 = rpb.reshape(NA_HEADS, N_DR * N_DC)
    return pl.pallas_call(
        _bias_kernel,
        grid=(NA_HEADS,),
        in_specs=[pl.BlockSpec(memory_space=pltpu.SMEM)],
        out_specs=pl.BlockSpec((1, N_TAB, GRID_W, 2 * GRID_W), lambda h: (h, 0, 0, 0)),
        out_shape=jax.ShapeDtypeStruct((NA_HEADS, N_TAB, GRID_W, 2 * GRID_W), F32),
        compiler_params=_params("arbitrary"),
        name="bias_table",
    )(rpb2)


_NT = (((1,), (1,)), ((), ()))


ATT_SB = 4
ATT_NSUB = 2
ATT_RB = ATT_SB * ATT_NSUB
ATT_WIN = ATT_SB + NA_KH
ATT_NBLK = ROWS // ATT_RB
LANES = 2 * GRID_W


def _band(first_row):
    win0 = min(max(first_row - NA_KH // 2, 0), ROWS - ATT_WIN)
    offs = tuple(min(max(first_row + a - NA_KH // 2, 0), ROWS - NA_KH) - win0 for a in range(ATT_SB))
    return offs, win0 - first_row + NA_KH - 1


def _band_tiles(a, off, c):
    p0, p1 = off // 2, (off + NA_KH - 1) // 2
    tiles = []
    for p in range(p0, p1 + 1):
        left = off <= 2 * p < off + NA_KH
        right = off <= 2 * p + 1 < off + NA_KH
        d = 2 * p - a + c
        tiles.append(TAB_BOTH + d if left and right else TAB_LEFT + d if left else TAB_RIGHT + d + 1)
    return p0, p1, tiles


def _attn_step(q_ref, k_ref, v_ref, kc_ref, vc_ref, tab_ref, o_ref, s_scr, sc_scr, p_scr, pc_scr, geoms):
    blk = pl.program_id(1)
    scale = HEAD_DIM ** -0.5
    nq = ATT_SB * GRID_W
    nk = ATT_WIN * GRID_W
    for t, (offs, c) in enumerate(geoms):
        first_row = (blk * ATT_NSUB + t) * ATT_SB
        w0 = jnp.clip(first_row - NA_KH // 2, 0, ROWS - ATT_WIN)
        start = pl.multiple_of(w0 * GRID_W, GRID_W)
        qrows = slice(t * nq, (t + 1) * nq)
        q = q_ref[qrows, :]
        s_scr[t] = lax.dot_general(q, k_ref[pl.ds(start, nk), :], _NT, preferred_element_type=F32)
        sc_scr[t] = lax.dot_general(q, kc_ref[...], _NT, preferred_element_type=F32)
        denoms = []
        for a in range(ATT_SB):
            rows = slice(a * GRID_W, (a + 1) * GRID_W)
            p0, p1, tiles = _band_tiles(a, offs[a], c)
            lo, hi = p0 * LANES, (p1 + 1) * LANES
            bias = jnp.concatenate([tab_ref[0, i] for i in tiles], axis=1)
            s_loc = s_scr[t, rows, lo:hi] * scale + bias
            s_ctx = sc_scr[t, rows, :] * scale
            m = jnp.maximum(jnp.max(s_loc, axis=-1, keepdims=True), jnp.max(s_ctx, axis=-1, keepdims=True))
            p_loc = jnp.exp(s_loc - m)
            p_ctx = jnp.exp(s_ctx - m)
            denoms.append(jnp.sum(p_loc, axis=-1, keepdims=True) + jnp.sum(p_ctx, axis=-1, keepdims=True))
            if lo > 0:
                p_scr[t, rows, :lo] = jnp.zeros((GRID_W, lo), BF16)
            p_scr[t, rows, lo:hi] = p_loc.astype(BF16)
            if hi < nk:
                p_scr[t, rows, hi:] = jnp.zeros((GRID_W, nk - hi), BF16)
            pc_scr[t, rows, :] = p_ctx.astype(BF16)
        o = jnp.dot(p_scr[t], v_ref[pl.ds(start, nk), :], preferred_element_type=F32)
        o = o + jnp.dot(pc_scr[t], vc_ref[...], preferred_element_type=F32)
        o_ref[qrows, :] = (o / jnp.concatenate(denoms, axis=0)).astype(o_ref.dtype)


def _attn_kernel(*refs):
    blk = pl.program_id(1)
    first, mid, last = _band(0), _band(ATT_SB), _band(ROWS - ATT_SB)

    @pl.when(blk == 0)
    def _():
        _attn_step(*refs, (first,) + (mid,) * (ATT_NSUB - 1))

    @pl.when((blk > 0) & (blk < ATT_NBLK - 1))
    def _():
        _attn_step(*refs, (mid,) * ATT_NSUB)

    @pl.when(blk == ATT_NBLK - 1)
    def _():
        _attn_step(*refs, (mid,) * (ATT_NSUB - 1) + (last,))


def _attention(qkv, kvc, tab):
    n = qkv.shape[0]
    c = kvc.shape[0]
    nq = ATT_SB * GRID_W
    nk = ATT_WIN * GRID_W
    return pl.pallas_call(
        _attn_kernel,
        grid=(NA_HEADS, ATT_NBLK),
        in_specs=[
            pl.BlockSpec((ATT_NSUB * nq, HEAD_DIM), lambda h, b: (b, h)),
            pl.BlockSpec((n, HEAD_DIM), lambda h, b: (0, NA_HEADS + h)),
            pl.BlockSpec((n, HEAD_DIM), lambda h, b: (0, 2 * NA_HEADS + h)),
            pl.BlockSpec((c, HEAD_DIM), lambda h, b: (0, h)),
            pl.BlockSpec((c, HEAD_DIM), lambda h, b: (0, NA_HEADS + h)),
            pl.BlockSpec((1, N_TAB, GRID_W, LANES), lambda h, b: (h, 0, 0, 0)),
        ],
        out_specs=pl.BlockSpec((ATT_NSUB * nq, HEAD_DIM), lambda h, b: (b, h)),
        out_shape=jax.ShapeDtypeStruct((n, NA_WIDTH), BF16),
        scratch_shapes=[
            pltpu.VMEM((ATT_NSUB, nq, nk), F32),
            pltpu.VMEM((ATT_NSUB, nq, c), F32),
            pltpu.VMEM((ATT_NSUB, nq, nk), BF16),
            pltpu.VMEM((ATT_NSUB, nq, c), BF16),
        ],
        compiler_params=_params("arbitrary", "arbitrary"),
        name="attention",
    )(qkv, qkv, qkv, kvc, kvc, tab)


def _outproj_kernel(ya_ref, yb_ref, w_ref, x_ref, gt_ref, o_ref):
    acc = jnp.dot(ya_ref[...], w_ref[:A_WIDTH, :], preferred_element_type=F32)
    acc = acc + jnp.dot(yb_ref[...], w_ref[A_WIDTH:, :], preferred_element_type=F32)
    o_ref[...] = x_ref[...] + gt_ref[...] * acc


def _outproj(ya, yb, w, x, gt, bm=1024, bn=512):
    m = ya.shape[0]
    k, n = w.shape
    return pl.pallas_call(
        _outproj_kernel,
        grid=(m // bm, n // bn),
        in_specs=[
            pl.BlockSpec((bm, A_WIDTH), lambda i, j: (i, 0)),
            pl.BlockSpec((bm, NA_WIDTH), lambda i, j: (i, 0)),
            pl.BlockSpec((k, bn), lambda i, j: (0, j)),
            pl.BlockSpec((bm, bn), lambda i, j: (i, j)),
            pl.BlockSpec((1, bn), lambda i, j: (0, j)),
        ],
        out_specs=pl.BlockSpec((bm, bn), lambda i, j: (i, j)),
        out_shape=jax.ShapeDtypeStruct((m, n), F32),
        compiler_params=_params("arbitrary", "arbitrary"),
        name="outproj",
    )(ya, yb, w, x, gt)


HALO = BF16_SUBLANES


def _upproj_kernel(h_ref, wa_ref, wg_ref, cw_ref, cb_ref, o_ref, a_scr, *, bm):
    a_scr[...] = jnp.dot(h_ref[...], wa_ref[...], preferred_element_type=F32)
    g = jnp.dot(h_ref[HALO:HALO + bm, :], wg_ref[...], preferred_element_type=F32)
    a_prev = a_scr[pl.ds(HALO - 1, bm), :]
    a_mid = a_scr[pl.ds(HALO, bm), :]
    a_next = a_scr[pl.ds(HALO + 1, bm), :]
    a = cb_ref[...] + ((a_prev * cw_ref[0:1, :] + a_mid * cw_ref[1:2, :]) + a_next * cw_ref[2:3, :])
    o_ref[...] = ((a * jax.nn.sigmoid(a)) * g).astype(o_ref.dtype)


def _upproj(h2p, w, cw, cb, m, pad_rows, col0, ncols, bn, bm=1024):
    k = h2p.shape[1]
    n = w.shape[1] // 2

    def col(j):
        return pl.multiple_of(col0 + j * bn, LANES)

    kern = functools.partial(_upproj_kernel, bm=bm)
    return pl.pallas_call(
        kern,
        grid=(m // bm, ncols // bn),
        in_specs=[
            pl.BlockSpec((pl.Element(bm + 2 * HALO), pl.Element(k)),
                         lambda i, j: (pl.multiple_of(pad_rows - HALO + i * bm, HALO), 0)),
            pl.BlockSpec((pl.Element(k), pl.Element(bn)), lambda i, j: (0, col(j))),
            pl.BlockSpec((pl.Element(k), pl.Element(bn)), lambda i, j: (0, pl.multiple_of(n + col(j), LANES))),
            pl.BlockSpec((pl.Element(3), pl.Element(bn)), lambda i, j: (0, col(j))),
            pl.BlockSpec((pl.Element(1), pl.Element(bn)), lambda i, j: (0, col(j))),
        ],
        out_specs=pl.BlockSpec((bm, bn), lambda i, j: (i, j)),
        out_shape=jax.ShapeDtypeStruct((m, ncols), BF16),
        scratch_shapes=[pltpu.VMEM((bm + 2 * HALO, bn), F32)],
        compiler_params=_params("arbitrary", "arbitrary"),
        name="upproj",
    )(h2p, w, w, cw, cb)


def _downproj_kernel(a0_ref, a1_ref, w_ref, x_ref, gt_ref, o_ref):
    k0 = a0_ref.shape[1]
    acc = jnp.dot(a0_ref[...], w_ref[:k0, :], preferred_element_type=F32)
    acc = acc + jnp.dot(a1_ref[...], w_ref[k0:, :], preferred_element_type=F32)
    o_ref[...] = x_ref[...] + gt_ref[...] * acc


def _downproj(a0, a1, w, x, gt, bm=512, bn=512):
    m, k0 = a0.shape
    k1 = a1.shape[1]
    k, n = w.shape
    assert k == k0 + k1
    return pl.pallas_call(
        _downproj_kernel,
        grid=(m // bm, n // bn),
        in_specs=[
            pl.BlockSpec((bm, k0), lambda i, j: (i, 0)),
            pl.BlockSpec((bm, k1), lambda i, j: (i, 0)),
            pl.BlockSpec((k, bn), lambda i, j: (0, j)),
            pl.BlockSpec((bm, bn), lambda i, j: (i, j)),
            pl.BlockSpec((1, bn), lambda i, j: (0, j)),
        ],
        out_specs=pl.BlockSpec((bm, bn), lambda i, j: (i, j)),
        out_shape=jax.ShapeDtypeStruct((m, n), F32),
        compiler_params=_params("arbitrary", "arbitrary"),
        name="downproj",
    )(a0, a1, w, x, gt)


def kernel(x, c, ctx, c_ctx, w_ada, b_ada, g_norm1, w_in, a_ln_g, a_ln_b, a_w_s, a_b_s, na_rpb, w_out,
           g_norm2, w_up, conv_w, conv_b, w_down, g_final):
    d = D_MODEL
    x2 = x[0]
    ctx2 = ctx[0]

    w_in_b = w_in[0].astype(BF16)
    w_out_b = w_out[0].astype(BF16)
    w_up_b = w_up[0].astype(BF16)
    w_down_b = w_down[0].astype(BF16)
    w_s_b = a_w_s[0].astype(BF16)
    b_s = a_b_s[0].reshape(A_GROUPS, CHUNK, 1)

    cs = jnp.zeros((BF16_SUBLANES, d), F32).at[0].set(c[0]).at[1].set(c_ctx)
    mod = _adaln(cs, w_ada[0], b_ada[0].reshape(1, 6 * d))
    sh1, sc1, gt1, sh2, sc2, gt2 = [mod[0:1, t * d:(t + 1) * d] for t in range(6)]
    csh1, csc1 = mod[1:2, 0:d], mod[1:2, d:2 * d]

    g1 = g_norm1[0].reshape(1, d)
    h = _norm_mod(x2, g1, sc1, sh1)
    hc = _norm_mod(ctx2, g1, csc1, csh1)

    y_a = _gmlp(h, w_in_b, a_ln_g[0].reshape(1, A_WIDTH), a_ln_b[0].reshape(1, A_WIDTH), w_s_b, b_s)
    qkv = _proj(h, w_in_b, 2 * A_WIDTH, 3 * NA_WIDTH, bm=1024, bn=1024)
    kvc = _proj(hc, w_in_b, KV_START, 2 * NA_WIDTH, bm=CTX_LEN, bn=1024)
    tab = _bias_table(na_rpb[0])
    y_b = _attention(qkv, kvc, tab)

    x_new = _outproj(y_a, y_b, w_out_b, x2, gt1)
    norm_bm = 256
    h2p = _norm_mod(x_new, g_norm2[0].reshape(1, d), sc2, sh2, bm=norm_bm, pad=True)
    ff_main = (D_FF // 512) * 512
    up = functools.partial(_upproj, h2p, w_up_b, conv_w[0], conv_b[0].reshape(1, D_FF), m=SEQ, pad_rows=norm_bm)
    hid0 = up(col0=0, ncols=ff_main, bn=512)
    hid1 = up(col0=ff_main, ncols=D_FF - ff_main, bn=D_FF - ff_main)
    x_fin = _downproj(hid0, hid1, w_down_b, x_new, gt2)
    out = _final_norm(x_fin, g_final.reshape(1, d))
    return out[None]
```

```python
import functools
import math
from typing import NamedTuple

import jax
import jax.numpy as jnp
from jax import lax
from jax.experimental import pallas as pl
from jax.experimental.pallas import tpu as pltpu

F32 = jnp.float32
BF16 = jnp.bfloat16

D_MODEL = 4096
SEQ = 8192
GRID_W = 64
ROWS = SEQ // GRID_W
CTX_LEN = 256
CHUNK = 128
HEAD_DIM = 128
A_WIDTH = 2048
A_GROUPS = A_WIDTH // HEAD_DIM
NA_WIDTH = 2048
NA_HEADS = NA_WIDTH // HEAD_DIM
NA_KH = 8
NA_KW = 16
IN_COLS = 2 * A_WIDTH + 3 * NA_WIDTH
KV_START = 2 * A_WIDTH + NA_WIDTH
D_FF = 11008
EPS = 1e-6

BF16_SUBLANES = 16
VMEM_LIMIT = 56 * 1024 * 1024


def _params(*sem):
    return pltpu.CompilerParams(dimension_semantics=sem, vmem_limit_bytes=VMEM_LIMIT)


def _ada_kernel(c_ref, w_ref, b_ref, o_ref):
    c = c_ref[...]
    a = (c * jax.nn.sigmoid(c)).astype(BF16)
    w = w_ref[...].astype(BF16)
    o_ref[...] = jnp.dot(a, w, preferred_element_type=F32) + b_ref[...]


def _adaln(cs, w, b, bn=512):
    rows, d = cs.shape
    n = w.shape[1]
    return pl.pallas_call(
        _ada_kernel,
        grid=(n // bn,),
        in_specs=[
            pl.BlockSpec((rows, d), lambda j: (0, 0)),
            pl.BlockSpec((d, bn), lambda j: (0, j)),
            pl.BlockSpec((1, bn), lambda j: (0, j)),
        ],
        out_specs=pl.BlockSpec((rows, bn), lambda j: (0, j)),
        out_shape=jax.ShapeDtypeStruct((rows, n), F32),
        compiler_params=_params("arbitrary"),
        name="adaln",
    )(cs, w, b)


def _norm_mod_kernel(x_ref, g_ref, sc_ref, sh_ref, o_ref, *, pad, nblk):
    def body():
        x = x_ref[...]
        y = x * lax.rsqrt(jnp.mean(x * x, axis=-1, keepdims=True) + EPS)
        y = y * g_ref[...]
        o_ref[...] = (y * (1.0 + sc_ref[...]) + sh_ref[...]).astype(o_ref.dtype)

    if pad:
        i = pl.program_id(0)
        is_pad = (i == 0) | (i == nblk + 1)

        @pl.when(is_pad)
        def _():
            o_ref[...] = jnp.zeros(o_ref.shape, o_ref.dtype)

        pl.when(jnp.logical_not(is_pad))(body)
    else:
        body()


def _norm_mod(x, g, sc, sh, bm=256, pad=False):
    m, d = x.shape
    nblk = m // bm
    vec = pl.BlockSpec((1, d), lambda i: (0, 0))
    if pad:
        x_spec = pl.BlockSpec((bm, d), lambda i: (jnp.clip(i - 1, 0, nblk - 1), 0))
        steps, rows = nblk + 2, m + 2 * bm
    else:
        x_spec = pl.BlockSpec((bm, d), lambda i: (i, 0))
        steps, rows = nblk, m
    return pl.pallas_call(
        functools.partial(_norm_mod_kernel, pad=pad, nblk=nblk),
        grid=(steps,),
        in_specs=[x_spec, vec, vec, vec],
        out_specs=pl.BlockSpec((bm, d), lambda i: (i, 0)),
        out_shape=jax.ShapeDtypeStruct((rows, d), BF16),
        compiler_params=_params("arbitrary"),
        name="norm_mod",
    )(x, g, sc, sh)


def _norm_kernel(x_ref, g_ref, o_ref):
    x = x_ref[...]
    y = x * lax.rsqrt(jnp.mean(x * x, axis=-1, keepdims=True) + EPS)
    o_ref[...] = y * g_ref[...]


def _final_norm(x, g, bm=256):
    m, d = x.shape
    return pl.pallas_call(
        _norm_kernel,
        grid=(m // bm,),
        in_specs=[pl.BlockSpec((bm, d), lambda i: (i, 0)), pl.BlockSpec((1, d), lambda i: (0, 0))],
        out_specs=pl.BlockSpec((bm, d), lambda i: (i, 0)),
        out_shape=jax.ShapeDtypeStruct((m, d), F32),
        compiler_params=_params("arbitrary"),
        name="final_norm",
    )(x, g)


class _SideCast(NamedTuple):
    src: jax.Array
    block: tuple[int, int]
    first: tuple[int, int]
    count: int
    axis: int

    def specs(self, nj):
        def step(i, j):
            return jnp.minimum(i * nj + j, self.count - 1)

        def src_idx(i, j):
            s = step(i, j)
            return (self.first[0] + s, self.first[1]) if self.axis == 0 else (self.first[0], self.first[1] + s)

        def dst_idx(i, j):
            s = step(i, j)
            return (s, 0) if self.axis == 0 else (0, s)

        rows = self.block[0] * (self.count if self.axis == 0 else 1)
        cols = self.block[1] * (self.count if self.axis == 1 else 1)
        return (pl.BlockSpec(self.block, src_idx), pl.BlockSpec(self.block, dst_idx),
                jax.ShapeDtypeStruct((rows, cols), BF16))


def _side_cast(src_ref, dst_ref):
    dst_ref[...] = src_ref[...].astype(BF16)


def _proj_kernel(a_ref, w_ref, *rest):
    if len(rest) == 3:
        _side_cast(rest[0], rest[2])
    o_ref = rest[-2] if len(rest) == 3 else rest[0]
    o_ref[...] = jnp.dot(a_ref[...], w_ref[...], preferred_element_type=F32).astype(o_ref.dtype)


def _proj(a, w, col0, n, bm, bn, side=None):
    m, k = a.shape
    off = col0 // bn
    nj = n // bn
    in_specs = [
        pl.BlockSpec((bm, k), lambda i, j: (i, 0)),
        pl.BlockSpec((k, bn), lambda i, j: (0, j + off)),
    ]
    out_specs = [pl.BlockSpec((bm, bn), lambda i, j: (i, j))]
    out_shape = [jax.ShapeDtypeStruct((m, n), BF16)]
    args = [a, w]
    if side is not None:
        assert (m // bm) * nj >= side.count
        s_in, s_out, s_shape = side.specs(nj)
        in_specs.append(s_in)
        out_specs.append(s_out)
        out_shape.append(s_shape)
        args.append(side.src)
    out = pl.pallas_call(
        _proj_kernel,
        grid=(m // bm, nj),
        in_specs=in_specs,
        out_specs=out_specs,
        out_shape=out_shape,
        compiler_params=_params("arbitrary", "arbitrary"),
        name="proj",
    )(*args)
    return out if side is not None else out[0]


def _gelu(x):
    return x * (0.5 * (1.0 + jnp.tanh(math.sqrt(2.0 / math.pi) * (x + 0.044715 * (x * x * x)))))


def _gmlp_kernel(h_ref, wu_ref, wv_ref, lng_ref, lnb_ref, ws_ref, bs_ref, side_src_ref, o_ref, side_dst_ref,
                 u_scr, v_scr, s1_scr, s2_scr, *, nj, bn, bm):
    j = pl.program_id(1)
    _side_cast(side_src_ref, side_dst_ref)
    h = h_ref[...]
    gu = _gelu(jnp.dot(h, wu_ref[...], preferred_element_type=F32))
    gv = _gelu(jnp.dot(h, wv_ref[...], preferred_element_type=F32))
    col = pl.multiple_of(j * bn, bn)
    u_scr[:, pl.ds(col, bn)] = gu
    v_scr[:, pl.ds(col, bn)] = gv
    ps1 = jnp.sum(gv, axis=-1, keepdims=True)
    ps2 = jnp.sum(gv * gv, axis=-1, keepdims=True)

    @pl.when(j == 0)
    def _():
        s1_scr[...] = ps1
        s2_scr[...] = ps2

    @pl.when(j > 0)
    def _():
        s1_scr[...] += ps1
        s2_scr[...] += ps2

    @pl.when(j == nj - 1)
    def _():
        mu = s1_scr[...] * (1.0 / A_WIDTH)
        var = s2_scr[...] * (1.0 / A_WIDTH) - mu * mu
        rstd = lax.rsqrt(var + EPS)
        nchunk = bm // CHUNK
        for g in range(A_GROUPS):
            cs = slice(g * HEAD_DIM, (g + 1) * HEAD_DIM)
            vn = (v_scr[:, cs] - mu) * rstd * lng_ref[:, cs] + lnb_ref[:, cs]
            vn = vn.astype(BF16)
            vcat = jnp.concatenate([vn[c * CHUNK:(c + 1) * CHUNK, :] for c in range(nchunk)], axis=1)
            s = jnp.dot(ws_ref[g], vcat, preferred_element_type=F32) + bs_ref[g]
            for c in range(nchunk):
                rs = slice(c * CHUNK, (c + 1) * CHUNK)
                o_ref[rs, cs] = (u_scr[rs, cs] * s[:, c * HEAD_DIM:(c + 1) * HEAD_DIM]).astype(o_ref.dtype)


def _gmlp(h, w_in, ln_g, ln_b, w_s, b_s, side, bm=512, bn=512):
    m, k = h.shape
    nj = A_WIDTH // bn
    assert (m // bm) * nj >= side.count
    s_in, s_out, s_shape = side.specs(nj)
    kern = functools.partial(_gmlp_kernel, nj=nj, bn=bn, bm=bm)
    return pl.pallas_call(
        kern,
        grid=(m // bm, nj),
        in_specs=[
            pl.BlockSpec((bm, k), lambda i, j: (i, 0)),
            pl.BlockSpec((k, bn), lambda i, j: (0, j)),
            pl.BlockSpec((k, bn), lambda i, j: (0, j + nj)),
            pl.BlockSpec((1, A_WIDTH), lambda i, j: (0, 0)),
            pl.BlockSpec((1, A_WIDTH), lambda i, j: (0, 0)),
            pl.BlockSpec((A_GROUPS, CHUNK, CHUNK), lambda i, j: (0, 0, 0)),
            pl.BlockSpec((A_GROUPS, CHUNK, 1), lambda i, j: (0, 0, 0)),
            s_in,
        ],
        out_specs=[pl.BlockSpec((bm, A_WIDTH), lambda i, j: (i, 0)), s_out],
        out_shape=[jax.ShapeDtypeStruct((m, A_WIDTH), BF16), s_shape],
        scratch_shapes=[
            pltpu.VMEM((bm, A_WIDTH), F32),
            pltpu.VMEM((bm, A_WIDTH), F32),
            pltpu.VMEM((bm, 1), F32),
            pltpu.VMEM((bm, 1), F32),
        ],
        compiler_params=_params("arbitrary", "arbitrary"),
        name="gmlp",
    )(h, w_in, w_in, ln_g, ln_b, w_s, b_s, side.src)


N_DR = 2 * NA_KH - 1
N_DC = 2 * NA_KW - 1
TAB_BOTH = 0
TAB_LEFT = N_DR - 1
TAB_RIGHT = 2 * N_DR - 1
N_TAB = 3 * N_DR - 1


def _bias_kernel(rpb_ref, o_ref):
    h = pl.program_id(0)
    qc = lax.broadcasted_iota(jnp.int32, (GRID_W, 2 * GRID_W), 0)
    lane = lax.broadcasted_iota(jnp.int32, (GRID_W, 2 * GRID_W), 1)
    kc = lane & (GRID_W - 1)
    hi = lane >= GRID_W
    dc = jnp.clip(kc - qc, -(NA_KW - 1), NA_KW - 1) + (NA_KW - 1)
    c0 = jnp.clip(qc - NA_KW // 2, 0, GRID_W - NA_KW)
    col_in = (kc >= c0) & (kc < c0 + NA_KW)
    hit = [dc == t for t in range(N_DC)]
    neg = jnp.full((GRID_W, 2 * GRID_W), -jnp.inf, F32)
    b = []
    for d in range(N_DR):
        acc = jnp.zeros((GRID_W, 2 * GRID_W), F32)
        for t in range(N_DC):
            acc = jnp.where(hit[t], rpb_ref[h, d * N_DC + t], acc)
        b.append(jnp.where(col_in, acc, neg))
    for d in range(N_DR):
        if d + 1 < N_DR:
            o_ref[0, TAB_BOTH + d] = jnp.where(hi, b[d + 1], b[d])
        o_ref[0, TAB_LEFT + d] = jnp.where(hi, neg, b[d])
        o_ref[0, TAB_RIGHT + d] = jnp.where(hi, b[d], neg)


def _bias_table(rpb):
    rpb2 = rpb.reshape(NA_HEADS, N_DR * N_DC)
    return pl.pallas_call(
        _bias_kernel,
        grid=(NA_HEADS,),
        in_specs=[pl.BlockSpec(memory_space=pltpu.SMEM)],
        out_specs=pl.BlockSpec((1, N_TAB, GRID_W, 2 * GRID_W), lambda h: (h, 0, 0, 0)),
        out_shape=jax.ShapeDtypeStruct((NA_HEADS, N_TAB, GRID_W, 2 * GRID_W), F32),
        compiler_params=_params("arbitrary"),
        name="bias_table",
    )(rpb2)


_NT = (((1,), (1,)), ((), ()))


ATT_SB = 4
ATT_NSUB = 4
ATT_RB = ATT_SB * ATT_NSUB
ATT_WIN = ATT_SB + NA_KH
ATT_NBLK = ROWS // ATT_RB
LANES = 2 * GRID_W


def _band(first_row):
    win0 = min(max(first_row - NA_KH // 2, 0), ROWS - ATT_WIN)
    offs = tuple(min(max(first_row + a - NA_KH // 2, 0), ROWS - NA_KH) - win0 for a in range(ATT_SB))
    return offs, win0 - first_row + NA_KH - 1


def _band_tiles(a, off, c):
    p0, p1 = off // 2, (off + NA_KH - 1) // 2
    tiles = []
    for p in range(p0, p1 + 1):
        left = off <= 2 * p < off + NA_KH
        right = off <= 2 * p + 1 < off + NA_KH
        d = 2 * p - a + c
        tiles.append(TAB_BOTH + d if left and right else TAB_LEFT + d if left else TAB_RIGHT + d + 1)
    return p0, p1, tiles


def _attn_step(q_ref, k_ref, v_ref, kc_ref, vc_ref, tab_ref, o_ref, s_scr, sc_scr, p_scr, pc_scr, geoms):
    blk = pl.program_id(1)
    scale = HEAD_DIM ** -0.5
    nq = ATT_SB * GRID_W
    nk = ATT_WIN * GRID_W
    for t, (offs, c) in enumerate(geoms):
        first_row = (blk * ATT_NSUB + t) * ATT_SB
        w0 = jnp.clip(first_row - NA_KH // 2, 0, ROWS - ATT_WIN)
        start = pl.multiple_of(w0 * GRID_W, GRID_W)
        qrows = slice(t * nq, (t + 1) * nq)
        q = q_ref[qrows, :]
        s_scr[t] = lax.dot_general(q, k_ref[pl.ds(start, nk), :], _NT, preferred_element_type=F32)
        sc_scr[t] = lax.dot_general(q, kc_ref[...], _NT, preferred_element_type=F32)
        denoms = []
        for a in range(ATT_SB):
            rows = slice(a * GRID_W, (a + 1) * GRID_W)
            p0, p1, tiles = _band_tiles(a, offs[a], c)
            lo, hi = p0 * LANES, (p1 + 1) * LANES
            bias = jnp.concatenate([tab_ref[0, i] for i in tiles], axis=1)
            s_loc = s_scr[t, rows, lo:hi] * scale + bias
            s_ctx = sc_scr[t, rows, :] * scale
            m = jnp.maximum(jnp.max(s_loc, axis=-1, keepdims=True), jnp.max(s_ctx, axis=-1, keepdims=True))
            p_loc = jnp.exp(s_loc - m)
            p_ctx = jnp.exp(s_ctx - m)
            denoms.append(jnp.sum(p_loc, axis=-1, keepdims=True) + jnp.sum(p_ctx, axis=-1, keepdims=True))
            if lo > 0:
                p_scr[t, rows, :lo] = jnp.zeros((GRID_W, lo), BF16)
            p_scr[t, rows, lo:hi] = p_loc.astype(BF16)
            if hi < nk:
                p_scr[t, rows, hi:] = jnp.zeros((GRID_W, nk - hi), BF16)
            pc_scr[t, rows, :] = p_ctx.astype(BF16)
        o = jnp.dot(p_scr[t], v_ref[pl.ds(start, nk), :], preferred_element_type=F32)
        o = o + jnp.dot(pc_scr[t], vc_ref[...], preferred_element_type=F32)
        o_ref[qrows, :] = (o / jnp.concatenate(denoms, axis=0)).astype(o_ref.dtype)


def _attn_kernel(*refs):
    blk = pl.program_id(1)
    first, mid, last = _band(0), _band(ATT_SB), _band(ROWS - ATT_SB)

    @pl.when(blk == 0)
    def _():
        _attn_step(*refs, (first,) + (mid,) * (ATT_NSUB - 1))

    @pl.when((blk > 0) & (blk < ATT_NBLK - 1))
    def _():
        _attn_step(*refs, (mid,) * ATT_NSUB)

    @pl.when(blk == ATT_NBLK - 1)
    def _():
        _attn_step(*refs, (mid,) * (ATT_NSUB - 1) + (last,))


def _attention(qkv, kvc, tab):
    n = qkv.shape[0]
    c = kvc.shape[0]
    nq = ATT_SB * GRID_W
    nk = ATT_WIN * GRID_W
    return pl.pallas_call(
        _attn_kernel,
        grid=(NA_HEADS, ATT_NBLK),
        in_specs=[
            pl.BlockSpec((ATT_NSUB * nq, HEAD_DIM), lambda h, b: (b, h)),
            pl.BlockSpec((n, HEAD_DIM), lambda h, b: (0, NA_HEADS + h)),
            pl.BlockSpec((n, HEAD_DIM), lambda h, b: (0, 2 * NA_HEADS + h)),
            pl.BlockSpec((c, HEAD_DIM), lambda h, b: (0, h)),
            pl.BlockSpec((c, HEAD_DIM), lambda h, b: (0, NA_HEADS + h)),
            pl.BlockSpec((1, N_TAB, GRID_W, LANES), lambda h, b: (h, 0, 0, 0)),
        ],
        out_specs=pl.BlockSpec((ATT_NSUB * nq, HEAD_DIM), lambda h, b: (b, h)),
        out_shape=jax.ShapeDtypeStruct((n, NA_WIDTH), BF16),
        scratch_shapes=[
            pltpu.VMEM((ATT_NSUB, nq, nk), F32),
            pltpu.VMEM((ATT_NSUB, nq, c), F32),
            pltpu.VMEM((ATT_NSUB, nq, nk), BF16),
            pltpu.VMEM((ATT_NSUB, nq, c), BF16),
        ],
        compiler_params=_params("arbitrary", "arbitrary"),
        name="attention",
    )(qkv, qkv, qkv, kvc, kvc, tab)


def _outproj_kernel(ya_ref, yb_ref, w_ref, x_ref, gt_ref, o_ref):
    acc = jnp.dot(ya_ref[...], w_ref[:A_WIDTH, :], preferred_element_type=F32)
    acc = acc + jnp.dot(yb_ref[...], w_ref[A_WIDTH:, :], preferred_element_type=F32)
    o_ref[...] = x_ref[...] + gt_ref[...] * acc


def _outproj(ya, yb, w, x, gt, bm=1024, bn=512):
    m = ya.shape[0]
    k, n = w.shape
    return pl.pallas_call(
        _outproj_kernel,
        grid=(m // bm, n // bn),
        in_specs=[
            pl.BlockSpec((bm, A_WIDTH), lambda i, j: (i, 0)),
            pl.BlockSpec((bm, NA_WIDTH), lambda i, j: (i, 0)),
            pl.BlockSpec((k, bn), lambda i, j: (0, j)),
            pl.BlockSpec((bm, bn), lambda i, j: (i, j)),
            pl.BlockSpec((1, bn), lambda i, j: (0, j)),
        ],
        out_specs=pl.BlockSpec((bm, bn), lambda i, j: (i, j)),
        out_shape=jax.ShapeDtypeStruct((m, n), F32),
        compiler_params=_params("arbitrary", "arbitrary"),
        name="outproj",
    )(ya, yb, w, x, gt)


HALO = BF16_SUBLANES


def _upproj_kernel(h_ref, wa_ref, wg_ref, cw_ref, cb_ref, *rest, bm):
    if len(rest) == 4:
        _side_cast(rest[0], rest[2])
        o_ref, a_scr = rest[1], rest[3]
    else:
        o_ref, a_scr = rest
    a_scr[...] = jnp.dot(h_ref[...], wa_ref[...], preferred_element_type=F32)
    g = jnp.dot(h_ref[HALO:HALO + bm, :], wg_ref[...], preferred_element_type=F32)
    a_prev = a_scr[pl.ds(HALO - 1, bm), :]
    a_mid = a_scr[pl.ds(HALO, bm), :]
    a_next = a_scr[pl.ds(HALO + 1, bm), :]
    a = cb_ref[...] + ((a_prev * cw_ref[0:1, :] + a_mid * cw_ref[1:2, :]) + a_next * cw_ref[2:3, :])
    o_ref[...] = ((a * jax.nn.sigmoid(a)) * g).astype(o_ref.dtype)


def _upproj(h2p, wa, wg, cw, cb, m, pad_rows, col0, ncols, bn, bm=1024, side=None):
    k = h2p.shape[1]
    off = col0 // bn
    nj = ncols // bn
    wspec = pl.BlockSpec((k, bn), lambda i, j: (0, off + j))
    in_specs = [
        pl.BlockSpec((pl.Element(bm + 2 * HALO), pl.Element(k)),
                     lambda i, j: (pl.multiple_of(pad_rows - HALO + i * bm, HALO), 0)),
        wspec,
        wspec,
        pl.BlockSpec((3, bn), lambda i, j: (0, off + j)),
        pl.BlockSpec((1, bn), lambda i, j: (0, off + j)),
    ]
    out_specs = [pl.BlockSpec((bm, bn), lambda i, j: (i, j))]
    out_shape = [jax.ShapeDtypeStruct((m, ncols), BF16)]
    args = [h2p, wa, wg, cw, cb]
    if side is not None:
        assert (m // bm) * nj >= side.count
        s_in, s_out, s_shape = side.specs(nj)
        in_specs.append(s_in)
        out_specs.append(s_out)
        out_shape.append(s_shape)
        args.append(side.src)
    out = pl.pallas_call(
        functools.partial(_upproj_kernel, bm=bm),
        grid=(m // bm, nj),
        in_specs=in_specs,
        out_specs=out_specs,
        out_shape=out_shape,
        scratch_shapes=[pltpu.VMEM((bm + 2 * HALO, bn), F32)],
        compiler_params=_params("arbitrary", "arbitrary"),
        name="upproj",
    )(*args)
    return out if side is not None else out[0]


def _downproj_kernel(a0_ref, a1_ref, w_ref, x_ref, gt_ref, o_ref):
    k0 = a0_ref.shape[1]
    acc = jnp.dot(a0_ref[...], w_ref[:k0, :], preferred_element_type=F32)
    acc = acc + jnp.dot(a1_ref[...], w_ref[k0:, :], preferred_element_type=F32)
    o_ref[...] = x_ref[...] + gt_ref[...] * acc


def _downproj(a0, a1, w, x, gt, bm=512, bn=512):
    m, k0 = a0.shape
    k1 = a1.shape[1]
    k, n = w.shape
    assert k == k0 + k1
    return pl.pallas_call(
        _downproj_kernel,
        grid=(m // bm, n // bn),
        in_specs=[
            pl.BlockSpec((bm, k0), lambda i, j: (i, 0)),
            pl.BlockSpec((bm, k1), lambda i, j: (i, 0)),
            pl.BlockSpec((k, bn), lambda i, j: (0, j)),
            pl.BlockSpec((bm, bn), lambda i, j: (i, j)),
            pl.BlockSpec((1, bn), lambda i, j: (0, j)),
        ],
        out_specs=pl.BlockSpec((bm, bn), lambda i, j: (i, j)),
        out_shape=jax.ShapeDtypeStruct((m, n), F32),
        compiler_params=_params("arbitrary", "arbitrary"),
        name="downproj",
    )(a0, a1, w, x, gt)


def kernel(x, c, ctx, c_ctx, w_ada, b_ada, g_norm1, w_in, a_ln_g, a_ln_b, a_w_s, a_b_s, na_rpb, w_out,
           g_norm2, w_up, conv_w, conv_b, w_down, g_final):
    d = D_MODEL
    x2 = x[0]
    ctx2 = ctx[0]

    w_in_b = w_in[0].astype(BF16)
    w_out_b = w_out[0].astype(BF16)
    w_s_b = a_w_s[0].astype(BF16)
    b_s = a_b_s[0].reshape(A_GROUPS, CHUNK, 1)

    cs = jnp.zeros((BF16_SUBLANES, d), F32).at[0].set(c[0]).at[1].set(c_ctx)
    mod = _adaln(cs, w_ada[0], b_ada[0].reshape(1, 6 * d))
    sh1, sc1, gt1, sh2, sc2, gt2 = [mod[0:1, t * d:(t + 1) * d] for t in range(6)]
    csh1, csc1 = mod[1:2, 0:d], mod[1:2, d:2 * d]

    g1 = g_norm1[0].reshape(1, d)
    h = _norm_mod(x2, g1, sc1, sh1)
    hc = _norm_mod(ctx2, g1, csc1, csh1)

    up_cols = 256
    cast_a = _SideCast(w_up[0], (d, up_cols), (0, 0), D_FF // up_cols, 1)
    cast_g = _SideCast(w_up[0], (d, up_cols), (0, D_FF // up_cols), D_FF // up_cols, 1)
    y_a, w_up_g = _gmlp(h, w_in_b, a_ln_g[0].reshape(1, A_WIDTH), a_ln_b[0].reshape(1, A_WIDTH), w_s_b, b_s, cast_g)
    qkv, w_up_a = _proj(h, w_in_b, 2 * A_WIDTH, 3 * NA_WIDTH, bm=1024, bn=1024, side=cast_a)
    kvc = _proj(hc, w_in_b, KV_START, 2 * NA_WIDTH, bm=CTX_LEN, bn=1024)
    tab = _bias_table(na_rpb[0])
    y_b = _attention(qkv, kvc, tab)

    x_new = _outproj(y_a, y_b, w_out_b, x2, gt1)
    norm_bm = 256
    h2p = _norm_mod(x_new, g_norm2[0].reshape(1, d), sc2, sh2, bm=norm_bm, pad=True)
    ff_main = (D_FF // 512) * 512
    down_rows = 128
    cast_d = _SideCast(w_down[0], (down_rows, d), (0, 0), D_FF // down_rows, 0)
    up = functools.partial(_upproj, h2p, w_up_a, w_up_g, conv_w[0], conv_b[0].reshape(1, D_FF), m=SEQ, pad_rows=norm_bm)
    hid0, w_down_b = up(col0=0, ncols=ff_main, bn=512, side=cast_d)
    hid1 = up(col0=ff_main, ncols=D_FF - ff_main, bn=D_FF - ff_main)
    x_fin = _downproj(hid0, hid1, w_down_b, x_new, gt2)
    out = _final_norm(x_fin, g_final.reshape(1, d))
    return out[None]
```

```python
import functools
import math
from typing import NamedTuple

import jax
import jax.numpy as jnp
from jax import lax
from jax.experimental import pallas as pl
from jax.experimental.pallas import tpu as pltpu

F32 = jnp.float32
BF16 = jnp.bfloat16

D_MODEL = 4096
SEQ = 8192
GRID_W = 64
ROWS = SEQ // GRID_W
CTX_LEN = 256
CHUNK = 128
HEAD_DIM = 128
A_WIDTH = 2048
A_GROUPS = A_WIDTH // HEAD_DIM
NA_WIDTH = 2048
NA_HEADS = NA_WIDTH // HEAD_DIM
NA_KH = 8
NA_KW = 16
IN_COLS = 2 * A_WIDTH + 3 * NA_WIDTH
KV_START = 2 * A_WIDTH + NA_WIDTH
D_FF = 11008
EPS = 1e-6

BF16_SUBLANES = 16
V7X_VMEM_BYTES = 64 * 1024 * 1024
VMEM_LIMIT = V7X_VMEM_BYTES - 4 * 1024 * 1024


def _params(*sem):
    return pltpu.CompilerParams(dimension_semantics=sem, vmem_limit_bytes=VMEM_LIMIT)


def _ada_kernel(c_ref, w_ref, b_ref, o_ref):
    c = c_ref[...]
    a = (c * jax.nn.sigmoid(c)).astype(BF16)
    w = w_ref[...].astype(BF16)
    o_ref[...] = jnp.dot(a, w, preferred_element_type=F32) + b_ref[...]


def _adaln(cs, w, b, bn=512):
    rows, d = cs.shape
    n = w.shape[1]
    return pl.pallas_call(
        _ada_kernel,
        grid=(n // bn,),
        in_specs=[
            pl.BlockSpec((rows, d), lambda j: (0, 0)),
            pl.BlockSpec((d, bn), lambda j: (0, j)),
            pl.BlockSpec((1, bn), lambda j: (0, j)),
        ],
        out_specs=pl.BlockSpec((rows, bn), lambda j: (0, j)),
        out_shape=jax.ShapeDtypeStruct((rows, n), F32),
        compiler_params=_params("arbitrary"),
        name="adaln",
    )(cs, w, b)


def _norm_mod_kernel(x_ref, g_ref, sc_ref, sh_ref, o_ref, *, pad, nblk):
    def body():
        x = x_ref[...]
        y = x * lax.rsqrt(jnp.mean(x * x, axis=-1, keepdims=True) + EPS)
        y = y * g_ref[...]
        o_ref[...] = (y * (1.0 + sc_ref[...]) + sh_ref[...]).astype(o_ref.dtype)

    if pad:
        i = pl.program_id(0)
        is_pad = (i == 0) | (i == nblk + 1)

        @pl.when(is_pad)
        def _():
            o_ref[...] = jnp.zeros(o_ref.shape, o_ref.dtype)

        pl.when(jnp.logical_not(is_pad))(body)
    else:
        body()


def _norm_mod(x, g, sc, sh, bm=256, pad=False):
    m, d = x.shape
    nblk = m // bm
    vec = pl.BlockSpec((1, d), lambda i: (0, 0))
    if pad:
        x_spec = pl.BlockSpec((bm, d), lambda i: (jnp.clip(i - 1, 0, nblk - 1), 0))
        steps, rows = nblk + 2, m + 2 * bm
    else:
        x_spec = pl.BlockSpec((bm, d), lambda i: (i, 0))
        steps, rows = nblk, m
    return pl.pallas_call(
        functools.partial(_norm_mod_kernel, pad=pad, nblk=nblk),
        grid=(steps,),
        in_specs=[x_spec, vec, vec, vec],
        out_specs=pl.BlockSpec((bm, d), lambda i: (i, 0)),
        out_shape=jax.ShapeDtypeStruct((rows, d), BF16),
        compiler_params=_params("arbitrary"),
        name="norm_mod",
    )(x, g, sc, sh)


def _norm_kernel(x_ref, g_ref, o_ref):
    x = x_ref[...]
    y = x * lax.rsqrt(jnp.mean(x * x, axis=-1, keepdims=True) + EPS)
    o_ref[...] = y * g_ref[...]


def _final_norm(x, g, bm=256):
    m, d = x.shape
    return pl.pallas_call(
        _norm_kernel,
        grid=(m // bm,),
        in_specs=[pl.BlockSpec((bm, d), lambda i: (i, 0)), pl.BlockSpec((1, d), lambda i: (0, 0))],
        out_specs=pl.BlockSpec((bm, d), lambda i: (i, 0)),
        out_shape=jax.ShapeDtypeStruct((m, d), F32),
        compiler_params=_params("arbitrary"),
        name="final_norm",
    )(x, g)


class _SideCast(NamedTuple):
    src: jax.Array
    block: tuple[int, int]
    first: tuple[int, int]
    count: int
    axis: int

    def specs(self, nj):
        def step(i, j):
            return jnp.minimum(i * nj + j, self.count - 1)

        def src_idx(i, j):
            s = step(i, j)
            return (self.first[0] + s, self.first[1]) if self.axis == 0 else (self.first[0], self.first[1] + s)

        def dst_idx(i, j):
            s = step(i, j)
            return (s, 0) if self.axis == 0 else (0, s)

        rows = self.block[0] * (self.count if self.axis == 0 else 1)
        cols = self.block[1] * (self.count if self.axis == 1 else 1)
        return (pl.BlockSpec(self.block, src_idx), pl.BlockSpec(self.block, dst_idx),
                jax.ShapeDtypeStruct((rows, cols), BF16))


def _side_cast(src_ref, dst_ref):
    dst_ref[...] = src_ref[...].astype(BF16)


def _proj_kernel(a_ref, w_ref, *rest):
    if len(rest) == 3:
        _side_cast(rest[0], rest[2])
    o_ref = rest[-2] if len(rest) == 3 else rest[0]
    o_ref[...] = jnp.dot(a_ref[...], w_ref[...], preferred_element_type=F32).astype(o_ref.dtype)


def _proj(a, w, col0, n, bm, bn, side=None):
    m, k = a.shape
    off = col0 // bn
    nj = n // bn
    in_specs = [
        pl.BlockSpec((bm, k), lambda i, j: (i, 0)),
        pl.BlockSpec((k, bn), lambda i, j: (0, j + off)),
    ]
    out_specs = [pl.BlockSpec((bm, bn), lambda i, j: (i, j))]
    out_shape = [jax.ShapeDtypeStruct((m, n), BF16)]
    args = [a, w]
    if side is not None:
        assert (m // bm) * nj >= side.count
        s_in, s_out, s_shape = side.specs(nj)
        in_specs.append(s_in)
        out_specs.append(s_out)
        out_shape.append(s_shape)
        args.append(side.src)
    out = pl.pallas_call(
        _proj_kernel,
        grid=(m // bm, nj),
        in_specs=in_specs,
        out_specs=out_specs,
        out_shape=out_shape,
        compiler_params=_params("arbitrary", "arbitrary"),
        name="proj",
    )(*args)
    return out if side is not None else out[0]


def _gelu(x):
    return x * (0.5 * (1.0 + jnp.tanh(math.sqrt(2.0 / math.pi) * (x + 0.044715 * (x * x * x)))))


def _gmlp_kernel(h_ref, w_ref, lng_ref, lnb_ref, ws_ref, bs_ref, side_src_ref, o_ref, side_dst_ref,
                 v_scr, s1_scr, s2_scr, *, nv, bn, bm):
    j = pl.program_id(1)
    _side_cast(side_src_ref, side_dst_ref)
    slab = bn // GMLP_SPLIT

    @pl.when(j < nv)
    def _():
        ps1 = ps2 = None
        for t in range(GMLP_SPLIT):
            gv = _gelu(jnp.dot(h_ref[...], w_ref[:, t * slab:(t + 1) * slab], preferred_element_type=F32))
            v_scr[:, pl.ds(pl.multiple_of(j * bn + t * slab, slab), slab)] = gv
            q1 = jnp.sum(gv, axis=-1, keepdims=True)
            q2 = jnp.sum(gv * gv, axis=-1, keepdims=True)
            ps1 = q1 if ps1 is None else ps1 + q1
            ps2 = q2 if ps2 is None else ps2 + q2
        first = j == 0
        s1_scr[...] = jnp.where(first, ps1, s1_scr[...] + ps1)
        s2_scr[...] = jnp.where(first, ps2, s2_scr[...] + ps2)

    @pl.when(j >= nv)
    def _():
        mu = s1_scr[...] * (1.0 / A_WIDTH)
        var = s2_scr[...] * (1.0 / A_WIDTH) - mu * mu
        rstd = lax.rsqrt(var + EPS)
        nchunk = bm // CHUNK
        for t in range(GMLP_SPLIT):
            gu = _gelu(jnp.dot(h_ref[...], w_ref[:, t * slab:(t + 1) * slab], preferred_element_type=F32))
            for gl in range(slab // HEAD_DIM):
                g = (j - nv) * (bn // HEAD_DIM) + (t * slab) // HEAD_DIM + gl
                gcols = pl.ds(pl.multiple_of(g * HEAD_DIM, HEAD_DIM), HEAD_DIM)
                vn = (v_scr[:, gcols] - mu) * rstd * lng_ref[:, gcols] + lnb_ref[:, gcols]
                vn = vn.astype(BF16)
                vcat = jnp.concatenate([vn[c * CHUNK:(c + 1) * CHUNK, :] for c in range(nchunk)], axis=1)
                s = jnp.dot(ws_ref[g], vcat, preferred_element_type=F32) + bs_ref[g]
                cs = slice(gl * HEAD_DIM, (gl + 1) * HEAD_DIM)
                os = slice(t * slab + gl * HEAD_DIM, t * slab + (gl + 1) * HEAD_DIM)
                for c in range(nchunk):
                    rs = slice(c * CHUNK, (c + 1) * CHUNK)
                    o_ref[rs, os] = (gu[rs, cs] * s[:, c * HEAD_DIM:(c + 1) * HEAD_DIM]).astype(o_ref.dtype)


GMLP_SPLIT = 2


def _gmlp(h, w_in, ln_g, ln_b, w_s, b_s, side, bm=1024, bn=1024):
    m, k = h.shape
    nv = A_WIDTH // bn
    nj = 2 * nv
    assert (m // bm) * nj >= side.count
    s_in, s_out, s_shape = side.specs(nj)
    kern = functools.partial(_gmlp_kernel, nv=nv, bn=bn, bm=bm)
    return pl.pallas_call(
        kern,
        grid=(m // bm, nj),
        in_specs=[
            pl.BlockSpec((bm, k), lambda i, j: (i, 0)),
            pl.BlockSpec((k, bn), lambda i, j: (0, jnp.where(j < nv, j + nv, j - nv))),
            pl.BlockSpec((1, A_WIDTH), lambda i, j: (0, 0)),
            pl.BlockSpec((1, A_WIDTH), lambda i, j: (0, 0)),
            pl.BlockSpec((A_GROUPS, CHUNK, CHUNK), lambda i, j: (0, 0, 0)),
            pl.BlockSpec((A_GROUPS, CHUNK, 1), lambda i, j: (0, 0, 0)),
            s_in,
        ],
        out_specs=[pl.BlockSpec((bm, bn), lambda i, j: (i, jnp.maximum(j - nv, 0))), s_out],
        out_shape=[jax.ShapeDtypeStruct((m, A_WIDTH), BF16), s_shape],
        scratch_shapes=[
            pltpu.VMEM((bm, A_WIDTH), F32),
            pltpu.VMEM((bm, 1), F32),
            pltpu.VMEM((bm, 1), F32),
        ],
        compiler_params=_params("arbitrary", "arbitrary"),
        name="gmlp",
    )(h, w_in, ln_g, ln_b, w_s, b_s, side.src)


N_DR = 2 * NA_KH - 1
N_DC = 2 * NA_KW - 1
TAB_BOTH = 0
TAB_LEFT = N_DR - 1
TAB_RIGHT = 2 * N_DR - 1
N_TAB = 3 * N_DR - 1


def _bias_kernel(rpb_ref, o_ref):
    h = pl.program_id(0)
    qc = lax.broadcasted_iota(jnp.int32, (GRID_W, 2 * GRID_W), 0)
    lane = lax.broadcasted_iota(jnp.int32, (GRID_W, 2 * GRID_W), 1)
    kc = lane & (GRID_W - 1)
    hi = lane >= GRID_W
    dc = jnp.clip(kc - qc, -(NA_KW - 1), NA_KW - 1) + (NA_KW - 1)
    c0 = jnp.clip(qc - NA_KW // 2, 0, GRID_W - NA_KW)
    col_in = (kc >= c0) & (kc < c0 + NA_KW)
    hit = [dc == t for t in range(N_DC)]
    neg = jnp.full((GRID_W, 2 * GRID_W), -jnp.inf, F32)
    b = []
    for d in range(N_DR):
        acc = jnp.zeros((GRID_W, 2 * GRID_W), F32)
        for t in range(N_DC):
            acc = jnp.where(hit[t], rpb_ref[h, d * N_DC + t], acc)
        b.append(jnp.where(col_in, acc, neg))
    for d in range(N_DR):
        if d + 1 < N_DR:
            o_ref[0, TAB_BOTH + d] = jnp.where(hi, b[d + 1], b[d])
        o_ref[0, TAB_LEFT + d] = jnp.where(hi, neg, b[d])
        o_ref[0, TAB_RIGHT + d] = jnp.where(hi, b[d], neg)


def _bias_table(rpb):
    rpb2 = rpb.reshape(NA_HEADS, N_DR * N_DC)
    return pl.pallas_call(
        _bias_kernel,
        grid=(NA_HEADS,),
        in_specs=[pl.BlockSpec(memory_space=pltpu.SMEM)],
        out_specs=pl.BlockSpec((1, N_TAB, GRID_W, 2 * GRID_W), lambda h: (h, 0, 0, 0)),
        out_shape=jax.ShapeDtypeStruct((NA_HEADS, N_TAB, GRID_W, 2 * GRID_W), F32),
        compiler_params=_params("arbitrary"),
        name="bias_table",
    )(rpb2)


_NT = (((1,), (1,)), ((), ()))


ATT_SB = 4
ATT_NSUB = 4
ATT_RB = ATT_SB * ATT_NSUB
ATT_WIN = ATT_SB + NA_KH
ATT_NBLK = ROWS // ATT_RB
LANES = 2 * GRID_W


def _band(first_row):
    win0 = min(max(first_row - NA_KH // 2, 0), ROWS - ATT_WIN)
    offs = tuple(min(max(first_row + a - NA_KH // 2, 0), ROWS - NA_KH) - win0 for a in range(ATT_SB))
    return offs, win0 - first_row + NA_KH - 1


def _band_tiles(a, off, c):
    p0, p1 = off // 2, (off + NA_KH - 1) // 2
    tiles = []
    for p in range(p0, p1 + 1):
        left = off <= 2 * p < off + NA_KH
        right = off <= 2 * p + 1 < off + NA_KH
        d = 2 * p - a + c
        tiles.append(TAB_BOTH + d if left and right else TAB_LEFT + d if left else TAB_RIGHT + d + 1)
    return p0, p1, tiles


def _attn_step(q_ref, k_ref, v_ref, kc_ref, vc_ref, tab_ref, o_ref, s_scr, sc_scr, p_scr, pc_scr, geoms):
    blk = pl.program_id(1)
    scale = HEAD_DIM ** -0.5
    nq = ATT_SB * GRID_W
    nk = ATT_WIN * GRID_W
    for t, (offs, c) in enumerate(geoms):
        first_row = (blk * ATT_NSUB + t) * ATT_SB
        w0 = jnp.clip(first_row - NA_KH // 2, 0, ROWS - ATT_WIN)
        start = pl.multiple_of(w0 * GRID_W, GRID_W)
        qrows = slice(t * nq, (t + 1) * nq)
        q = q_ref[qrows, :]
        s_scr[t] = lax.dot_general(q, k_ref[pl.ds(start, nk), :], _NT, preferred_element_type=F32)
        sc_scr[t] = lax.dot_general(q, kc_ref[...], _NT, preferred_element_type=F32)
        denoms = []
        for a in range(ATT_SB):
            rows = slice(a * GRID_W, (a + 1) * GRID_W)
            p0, p1, tiles = _band_tiles(a, offs[a], c)
            lo, hi = p0 * LANES, (p1 + 1) * LANES
            bias = jnp.concatenate([tab_ref[0, i] for i in tiles], axis=1)
            s_loc = s_scr[t, rows, lo:hi] * scale + bias
            s_ctx = sc_scr[t, rows, :] * scale
            m = jnp.maximum(jnp.max(s_loc, axis=-1, keepdims=True), jnp.max(s_ctx, axis=-1, keepdims=True))
            p_loc = jnp.exp(s_loc - m)
            p_ctx = jnp.exp(s_ctx - m)
            denoms.append(jnp.sum(p_loc, axis=-1, keepdims=True) + jnp.sum(p_ctx, axis=-1, keepdims=True))
            if lo > 0:
                p_scr[t, rows, :lo] = jnp.zeros((GRID_W, lo), BF16)
            p_scr[t, rows, lo:hi] = p_loc.astype(BF16)
            if hi < nk:
                p_scr[t, rows, hi:] = jnp.zeros((GRID_W, nk - hi), BF16)
            pc_scr[t, rows, :] = p_ctx.astype(BF16)
        o = jnp.dot(p_scr[t], v_ref[pl.ds(start, nk), :], preferred_element_type=F32)
        o = o + jnp.dot(pc_scr[t], vc_ref[...], preferred_element_type=F32)
        o_ref[qrows, :] = (o / jnp.concatenate(denoms, axis=0)).astype(o_ref.dtype)


def _attn_kernel(*refs):
    blk = pl.program_id(1)
    first, mid, last = _band(0), _band(ATT_SB), _band(ROWS - ATT_SB)

    @pl.when(blk == 0)
    def _():
        _attn_step(*refs, (first,) + (mid,) * (ATT_NSUB - 1))

    @pl.when((blk > 0) & (blk < ATT_NBLK - 1))
    def _():
        _attn_step(*refs, (mid,) * ATT_NSUB)

    @pl.when(blk == ATT_NBLK - 1)
    def _():
        _attn_step(*refs, (mid,) * (ATT_NSUB - 1) + (last,))


def _attention(qkv, kvc, tab):
    n = qkv.shape[0]
    c = kvc.shape[0]
    nq = ATT_SB * GRID_W
    nk = ATT_WIN * GRID_W
    return pl.pallas_call(
        _attn_kernel,
        grid=(NA_HEADS, ATT_NBLK),
        in_specs=[
            pl.BlockSpec((ATT_NSUB * nq, HEAD_DIM), lambda h, b: (b, h)),
            pl.BlockSpec((n, HEAD_DIM), lambda h, b: (0, NA_HEADS + h)),
            pl.BlockSpec((n, HEAD_DIM), lambda h, b: (0, 2 * NA_HEADS + h)),
            pl.BlockSpec((c, HEAD_DIM), lambda h, b: (0, h)),
            pl.BlockSpec((c, HEAD_DIM), lambda h, b: (0, NA_HEADS + h)),
            pl.BlockSpec((1, N_TAB, GRID_W, LANES), lambda h, b: (h, 0, 0, 0)),
        ],
        out_specs=pl.BlockSpec((ATT_NSUB * nq, HEAD_DIM), lambda h, b: (b, h)),
        out_shape=jax.ShapeDtypeStruct((n, NA_WIDTH), BF16),
        scratch_shapes=[
            pltpu.VMEM((ATT_NSUB, nq, nk), F32),
            pltpu.VMEM((ATT_NSUB, nq, c), F32),
            pltpu.VMEM((ATT_NSUB, nq, nk), BF16),
            pltpu.VMEM((ATT_NSUB, nq, c), BF16),
        ],
        compiler_params=_params("arbitrary", "arbitrary"),
        name="attention",
    )(qkv, qkv, qkv, kvc, kvc, tab)


def _outproj_kernel(ya_ref, yb_ref, w_ref, x_ref, gt_ref, side_src_ref, o_ref, side_dst_ref):
    _side_cast(side_src_ref, side_dst_ref)
    acc = jnp.dot(ya_ref[...], w_ref[:A_WIDTH, :], preferred_element_type=F32)
    acc = acc + jnp.dot(yb_ref[...], w_ref[A_WIDTH:, :], preferred_element_type=F32)
    o_ref[...] = x_ref[...] + gt_ref[...] * acc


def _outproj(ya, yb, w, x, gt, side, bm=1024, bn=512):
    m = ya.shape[0]
    k, n = w.shape
    nj = n // bn
    assert (m // bm) * nj >= side.count
    s_in, s_out, s_shape = side.specs(nj)
    return pl.pallas_call(
        _outproj_kernel,
        grid=(m // bm, nj),
        in_specs=[
            pl.BlockSpec((bm, A_WIDTH), lambda i, j: (i, 0)),
            pl.BlockSpec((bm, NA_WIDTH), lambda i, j: (i, 0)),
            pl.BlockSpec((k, bn), lambda i, j: (0, j)),
            pl.BlockSpec((bm, bn), lambda i, j: (i, j)),
            pl.BlockSpec((1, bn), lambda i, j: (0, j)),
            s_in,
        ],
        out_specs=[pl.BlockSpec((bm, bn), lambda i, j: (i, j)), s_out],
        out_shape=[jax.ShapeDtypeStruct((m, n), F32), s_shape],
        compiler_params=_params("arbitrary", "arbitrary"),
        name="outproj",
    )(ya, yb, w, x, gt, side.src)


HALO = BF16_SUBLANES


def _upproj_kernel(h_ref, wa_ref, wg_ref, cw_ref, cb_ref, *rest, bm):
    if len(rest) == 4:
        _side_cast(rest[0], rest[2])
        o_ref, a_scr = rest[1], rest[3]
    else:
        o_ref, a_scr = rest
    a_scr[...] = jnp.dot(h_ref[...], wa_ref[...], preferred_element_type=F32)
    g = jnp.dot(h_ref[HALO:HALO + bm, :], wg_ref[...], preferred_element_type=F32)
    a_prev = a_scr[pl.ds(HALO - 1, bm), :]
    a_mid = a_scr[pl.ds(HALO, bm), :]
    a_next = a_scr[pl.ds(HALO + 1, bm), :]
    a = cb_ref[...] + ((a_prev * cw_ref[0:1, :] + a_mid * cw_ref[1:2, :]) + a_next * cw_ref[2:3, :])
    o_ref[...] = ((a * jax.nn.sigmoid(a)) * g).astype(o_ref.dtype)


def _upproj(h2p, wa, wg, cw, cb, m, pad_rows, col0, ncols, bn, bm=1024, side=None):
    k = h2p.shape[1]
    off = col0 // bn
    nj = ncols // bn
    wspec = pl.BlockSpec((k, bn), lambda i, j: (0, off + j))
    in_specs = [
        pl.BlockSpec((pl.Element(bm + 2 * HALO), pl.Element(k)),
                     lambda i, j: (pl.multiple_of(pad_rows - HALO + i * bm, HALO), 0)),
        wspec,
        wspec,
        pl.BlockSpec((3, bn), lambda i, j: (0, off + j)),
        pl.BlockSpec((1, bn), lambda i, j: (0, off + j)),
    ]
    out_specs = [pl.BlockSpec((bm, bn), lambda i, j: (i, j))]
    out_shape = [jax.ShapeDtypeStruct((m, ncols), BF16)]
    args = [h2p, wa, wg, cw, cb]
    if side is not None:
        assert (m // bm) * nj >= side.count
        s_in, s_out, s_shape = side.specs(nj)
        in_specs.append(s_in)
        out_specs.append(s_out)
        out_shape.append(s_shape)
        args.append(side.src)
    out = pl.pallas_call(
        functools.partial(_upproj_kernel, bm=bm),
        grid=(m // bm, nj),
        in_specs=in_specs,
        out_specs=out_specs,
        out_shape=out_shape,
        scratch_shapes=[pltpu.VMEM((bm + 2 * HALO, bn), F32)],
        compiler_params=_params("arbitrary", "arbitrary"),
        name="upproj",
    )(*args)
    return out if side is not None else out[0]


def _downproj_kernel(a0_ref, a1_ref, w_ref, x_ref, gt_ref, o_ref):
    k0 = a0_ref.shape[1]
    acc = jnp.dot(a0_ref[...], w_ref[:k0, :], preferred_element_type=F32)
    acc = acc + jnp.dot(a1_ref[...], w_ref[k0:, :], preferred_element_type=F32)
    o_ref[...] = x_ref[...] + gt_ref[...] * acc


def _downproj(a0, a1, w, x, gt, bm=512, bn=512):
    m, k0 = a0.shape
    k1 = a1.shape[1]
    k, n = w.shape
    assert k == k0 + k1
    return pl.pallas_call(
        _downproj_kernel,
        grid=(m // bm, n // bn),
        in_specs=[
            pl.BlockSpec((bm, k0), lambda i, j: (i, 0)),
            pl.BlockSpec((bm, k1), lambda i, j: (i, 0)),
            pl.BlockSpec((k, bn), lambda i, j: (0, j)),
            pl.BlockSpec((bm, bn), lambda i, j: (i, j)),
            pl.BlockSpec((1, bn), lambda i, j: (0, j)),
        ],
        out_specs=pl.BlockSpec((bm, bn), lambda i, j: (i, j)),
        out_shape=jax.ShapeDtypeStruct((m, n), F32),
        compiler_params=_params("arbitrary", "arbitrary"),
        name="downproj",
    )(a0, a1, w, x, gt)


def kernel(x, c, ctx, c_ctx, w_ada, b_ada, g_norm1, w_in, a_ln_g, a_ln_b, a_w_s, a_b_s, na_rpb, w_out,
           g_norm2, w_up, conv_w, conv_b, w_down, g_final):
    d = D_MODEL
    x2 = x[0]
    ctx2 = ctx[0]

    w_in_b = w_in[0].astype(BF16)
    w_s_b = a_w_s[0].astype(BF16)
    b_s = a_b_s[0].reshape(A_GROUPS, CHUNK, 1)

    cs = jnp.zeros((BF16_SUBLANES, d), F32).at[0].set(c[0]).at[1].set(c_ctx)
    mod = _adaln(cs, w_ada[0], b_ada[0].reshape(1, 6 * d))
    sh1, sc1, gt1, sh2, sc2, gt2 = [mod[0:1, t * d:(t + 1) * d] for t in range(6)]
    csh1, csc1 = mod[1:2, 0:d], mod[1:2, d:2 * d]

    g1 = g_norm1[0].reshape(1, d)
    h = _norm_mod(x2, g1, sc1, sh1)
    hc = _norm_mod(ctx2, g1, csc1, csh1)

    up_cols = 256
    cast_a = _SideCast(w_up[0], (d, up_cols), (0, 0), D_FF // up_cols, 1)
    cast_g = _SideCast(w_up[0], (d, up_cols), (0, D_FF // up_cols), D_FF // up_cols, 1)
    out_cols = 128
    cast_o = _SideCast(w_out[0], (d, out_cols), (0, 0), d // out_cols, 1)
    y_a, w_out_b = _gmlp(h, w_in_b, a_ln_g[0].reshape(1, A_WIDTH), a_ln_b[0].reshape(1, A_WIDTH), w_s_b, b_s, cast_o)
    qkv, w_up_a = _proj(h, w_in_b, 2 * A_WIDTH, 3 * NA_WIDTH, bm=1024, bn=1024, side=cast_a)
    kvc = _proj(hc, w_in_b, KV_START, 2 * NA_WIDTH, bm=CTX_LEN, bn=1024)
    tab = _bias_table(na_rpb[0])
    y_b = _attention(qkv, kvc, tab)

    x_new, w_up_g = _outproj(y_a, y_b, w_out_b, x2, gt1, cast_g)
    norm_bm = 256
    h2p = _norm_mod(x_new, g_norm2[0].reshape(1, d), sc2, sh2, bm=norm_bm, pad=True)
    ff_main = (D_FF // 512) * 512
    down_rows = 128
    cast_d = _SideCast(w_down[0], (down_rows, d), (0, 0), D_FF // down_rows, 0)
    up = functools.partial(_upproj, h2p, w_up_a, w_up_g, conv_w[0], conv_b[0].reshape(1, D_FF), m=SEQ, pad_rows=norm_bm)
    hid0, w_down_b = up(col0=0, ncols=ff_main, bn=512, side=cast_d)
    hid1 = up(col0=ff_main, ncols=D_FF - ff_main, bn=D_FF - ff_main)
    x_fin = _downproj(hid0, hid1, w_down_b, x_new, gt2)
    out = _final_norm(x_fin, g_final.reshape(1, d))
    return out[None]
```

```python
import functools
import math
from typing import NamedTuple

import jax
import jax.numpy as jnp
from jax import lax
from jax.experimental import pallas as pl
from jax.experimental.pallas import tpu as pltpu

F32 = jnp.float32
BF16 = jnp.bfloat16

D_MODEL = 4096
SEQ = 8192
GRID_W = 64
ROWS = SEQ // GRID_W
CTX_LEN = 256
CHUNK = 128
HEAD_DIM = 128
A_WIDTH = 2048
A_GROUPS = A_WIDTH // HEAD_DIM
NA_WIDTH = 2048
NA_HEADS = NA_WIDTH // HEAD_DIM
NA_KH = 8
NA_KW = 16
IN_COLS = 2 * A_WIDTH + 3 * NA_WIDTH
KV_START = 2 * A_WIDTH + NA_WIDTH
D_FF = 11008
EPS = 1e-6

BF16_SUBLANES = 16
V7X_VMEM_BYTES = 64 * 1024 * 1024
VMEM_LIMIT = V7X_VMEM_BYTES - 4 * 1024 * 1024


def _params(*sem):
    return pltpu.CompilerParams(dimension_semantics=sem, vmem_limit_bytes=VMEM_LIMIT)


def _ada_kernel(c_ref, w_ref, b_ref, o_ref):
    c = c_ref[...]
    a = (c * jax.nn.sigmoid(c)).astype(BF16)
    w = w_ref[...].astype(BF16)
    o_ref[...] = jnp.dot(a, w, preferred_element_type=F32) + b_ref[...]


def _adaln(cs, w, b, bn=512):
    rows, d = cs.shape
    n = w.shape[1]
    return pl.pallas_call(
        _ada_kernel,
        grid=(n // bn,),
        in_specs=[
            pl.BlockSpec((rows, d), lambda j: (0, 0)),
            pl.BlockSpec((d, bn), lambda j: (0, j)),
            pl.BlockSpec((1, bn), lambda j: (0, j)),
        ],
        out_specs=pl.BlockSpec((rows, bn), lambda j: (0, j)),
        out_shape=jax.ShapeDtypeStruct((rows, n), F32),
        compiler_params=_params("arbitrary"),
        name="adaln",
    )(cs, w, b)


def _norm_mod_kernel(x_ref, g_ref, sc_ref, sh_ref, o_ref, *, pad, nblk):
    def body():
        x = x_ref[...]
        y = x * lax.rsqrt(jnp.mean(x * x, axis=-1, keepdims=True) + EPS)
        y = y * g_ref[...]
        o_ref[...] = (y * (1.0 + sc_ref[...]) + sh_ref[...]).astype(o_ref.dtype)

    if pad:
        i = pl.program_id(0)
        is_pad = (i == 0) | (i == nblk + 1)

        @pl.when(is_pad)
        def _():
            o_ref[...] = jnp.zeros(o_ref.shape, o_ref.dtype)

        pl.when(jnp.logical_not(is_pad))(body)
    else:
        body()


def _norm_mod(x, g, sc, sh, bm=256, pad=False):
    m, d = x.shape
    nblk = m // bm
    vec = pl.BlockSpec((1, d), lambda i: (0, 0))
    if pad:
        x_spec = pl.BlockSpec((bm, d), lambda i: (jnp.clip(i - 1, 0, nblk - 1), 0))
        steps, rows = nblk + 2, m + 2 * bm
    else:
        x_spec = pl.BlockSpec((bm, d), lambda i: (i, 0))
        steps, rows = nblk, m
    return pl.pallas_call(
        functools.partial(_norm_mod_kernel, pad=pad, nblk=nblk),
        grid=(steps,),
        in_specs=[x_spec, vec, vec, vec],
        out_specs=pl.BlockSpec((bm, d), lambda i: (i, 0)),
        out_shape=jax.ShapeDtypeStruct((rows, d), BF16),
        compiler_params=_params("arbitrary"),
        name="norm_mod",
    )(x, g, sc, sh)


def _norm_kernel(x_ref, g_ref, o_ref):
    x = x_ref[...]
    y = x * lax.rsqrt(jnp.mean(x * x, axis=-1, keepdims=True) + EPS)
    o_ref[...] = y * g_ref[...]


def _final_norm(x, g, bm=256):
    m, d = x.shape
    return pl.pallas_call(
        _norm_kernel,
        grid=(m // bm,),
        in_specs=[pl.BlockSpec((bm, d), lambda i: (i, 0)), pl.BlockSpec((1, d), lambda i: (0, 0))],
        out_specs=pl.BlockSpec((bm, d), lambda i: (i, 0)),
        out_shape=jax.ShapeDtypeStruct((m, d), F32),
        compiler_params=_params("arbitrary"),
        name="final_norm",
    )(x, g)


class _SideCast(NamedTuple):
    src: jax.Array
    block: tuple[int, int]
    first: tuple[int, int]
    count: int
    axis: int

    def specs(self, nj):
        def step(i, j):
            return jnp.minimum(i * nj + j, self.count - 1)

        def src_idx(i, j):
            s = step(i, j)
            return (self.first[0] + s, self.first[1]) if self.axis == 0 else (self.first[0], self.first[1] + s)

        def dst_idx(i, j):
            s = step(i, j)
            return (s, 0) if self.axis == 0 else (0, s)

        rows = self.block[0] * (self.count if self.axis == 0 else 1)
        cols = self.block[1] * (self.count if self.axis == 1 else 1)
        return (pl.BlockSpec(self.block, src_idx), pl.BlockSpec(self.block, dst_idx),
                jax.ShapeDtypeStruct((rows, cols), BF16))


def _side_cast(src_ref, dst_ref):
    dst_ref[...] = src_ref[...].astype(BF16)


def _proj_kernel(a_ref, w_ref, *rest):
    if len(rest) == 3:
        _side_cast(rest[0], rest[2])
    o_ref = rest[-2] if len(rest) == 3 else rest[0]
    o_ref[...] = jnp.dot(a_ref[...], w_ref[...], preferred_element_type=F32).astype(o_ref.dtype)


def _proj(a, w, col0, n, bm, bn, side=None):
    m, k = a.shape
    off = col0 // bn
    nj = n // bn
    in_specs = [
        pl.BlockSpec((bm, k), lambda i, j: (i, 0)),
        pl.BlockSpec((k, bn), lambda i, j: (0, j + off)),
    ]
    out_specs = [pl.BlockSpec((bm, bn), lambda i, j: (i, j))]
    out_shape = [jax.ShapeDtypeStruct((m, n), BF16)]
    args = [a, w]
    if side is not None:
        assert (m // bm) * nj >= side.count
        s_in, s_out, s_shape = side.specs(nj)
        in_specs.append(s_in)
        out_specs.append(s_out)
        out_shape.append(s_shape)
        args.append(side.src)
    out = pl.pallas_call(
        _proj_kernel,
        grid=(m // bm, nj),
        in_specs=in_specs,
        out_specs=out_specs,
        out_shape=out_shape,
        compiler_params=_params("arbitrary", "arbitrary"),
        name="proj",
    )(*args)
    return out if side is not None else out[0]


def _gelu(x):
    return x * (0.5 * (1.0 + jnp.tanh(math.sqrt(2.0 / math.pi) * (x + 0.044715 * (x * x * x)))))


def _gmlp_kernel(h_ref, w_ref, lng_ref, lnb_ref, ws_ref, bs_ref, side_src_ref, o_ref, side_dst_ref,
                 v_scr, s1_scr, s2_scr, *, nv, bn, bm):
    j = pl.program_id(1)
    _side_cast(side_src_ref, side_dst_ref)
    slab = bn // GMLP_SPLIT

    @pl.when(j < nv)
    def _():
        ps1 = ps2 = None
        for t in range(GMLP_SPLIT):
            gv = _gelu(jnp.dot(h_ref[...], w_ref[:, t * slab:(t + 1) * slab], preferred_element_type=F32))
            v_scr[:, pl.ds(pl.multiple_of(j * bn + t * slab, slab), slab)] = gv
            q1 = jnp.sum(gv, axis=-1, keepdims=True)
            q2 = jnp.sum(gv * gv, axis=-1, keepdims=True)
            ps1 = q1 if ps1 is None else ps1 + q1
            ps2 = q2 if ps2 is None else ps2 + q2
        first = j == 0
        s1_scr[...] = jnp.where(first, ps1, s1_scr[...] + ps1)
        s2_scr[...] = jnp.where(first, ps2, s2_scr[...] + ps2)

    @pl.when(j >= nv)
    def _():
        mu = s1_scr[...] * (1.0 / A_WIDTH)
        var = s2_scr[...] * (1.0 / A_WIDTH) - mu * mu
        rstd = lax.rsqrt(var + EPS)
        nchunk = bm // CHUNK
        for t in range(GMLP_SPLIT):
            gu = _gelu(jnp.dot(h_ref[...], w_ref[:, t * slab:(t + 1) * slab], preferred_element_type=F32))
            for gl in range(slab // HEAD_DIM):
                g = (j - nv) * (bn // HEAD_DIM) + (t * slab) // HEAD_DIM + gl
                gcols = pl.ds(pl.multiple_of(g * HEAD_DIM, HEAD_DIM), HEAD_DIM)
                vn = (v_scr[:, gcols] - mu) * rstd * lng_ref[:, gcols] + lnb_ref[:, gcols]
                vn = vn.astype(BF16)
                vcat = jnp.concatenate([vn[c * CHUNK:(c + 1) * CHUNK, :] for c in range(nchunk)], axis=1)
                s = jnp.dot(ws_ref[g], vcat, preferred_element_type=F32) + bs_ref[g]
                cs = slice(gl * HEAD_DIM, (gl + 1) * HEAD_DIM)
                os = slice(t * slab + gl * HEAD_DIM, t * slab + (gl + 1) * HEAD_DIM)
                for c in range(nchunk):
                    rs = slice(c * CHUNK, (c + 1) * CHUNK)
                    o_ref[rs, os] = (gu[rs, cs] * s[:, c * HEAD_DIM:(c + 1) * HEAD_DIM]).astype(o_ref.dtype)


GMLP_SPLIT = 2


def _gmlp(h, w_in, ln_g, ln_b, w_s, b_s, side, bm=1024, bn=1024):
    m, k = h.shape
    nv = A_WIDTH // bn
    nj = 2 * nv
    assert (m // bm) * nj >= side.count
    s_in, s_out, s_shape = side.specs(nj)
    kern = functools.partial(_gmlp_kernel, nv=nv, bn=bn, bm=bm)
    return pl.pallas_call(
        kern,
        grid=(m // bm, nj),
        in_specs=[
            pl.BlockSpec((bm, k), lambda i, j: (i, 0)),
            pl.BlockSpec((k, bn), lambda i, j: (0, jnp.where(j < nv, j + nv, j - nv))),
            pl.BlockSpec((1, A_WIDTH), lambda i, j: (0, 0)),
            pl.BlockSpec((1, A_WIDTH), lambda i, j: (0, 0)),
            pl.BlockSpec((A_GROUPS, CHUNK, CHUNK), lambda i, j: (0, 0, 0)),
            pl.BlockSpec((A_GROUPS, CHUNK, 1), lambda i, j: (0, 0, 0)),
            s_in,
        ],
        out_specs=[pl.BlockSpec((bm, bn), lambda i, j: (i, jnp.maximum(j - nv, 0))), s_out],
        out_shape=[jax.ShapeDtypeStruct((m, A_WIDTH), BF16), s_shape],
        scratch_shapes=[
            pltpu.VMEM((bm, A_WIDTH), F32),
            pltpu.VMEM((bm, 1), F32),
            pltpu.VMEM((bm, 1), F32),
        ],
        compiler_params=_params("arbitrary", "arbitrary"),
        name="gmlp",
    )(h, w_in, ln_g, ln_b, w_s, b_s, side.src)


N_DR = 2 * NA_KH - 1
N_DC = 2 * NA_KW - 1
TAB_BOTH = 0
TAB_LEFT = N_DR - 1
TAB_RIGHT = 2 * N_DR - 1
N_TAB = 3 * N_DR - 1


def _bias_kernel(rpb_ref, o_ref):
    h = pl.program_id(0)
    qc = lax.broadcasted_iota(jnp.int32, (GRID_W, 2 * GRID_W), 0)
    lane = lax.broadcasted_iota(jnp.int32, (GRID_W, 2 * GRID_W), 1)
    kc = lane & (GRID_W - 1)
    hi = lane >= GRID_W
    dc = jnp.clip(kc - qc, -(NA_KW - 1), NA_KW - 1) + (NA_KW - 1)
    c0 = jnp.clip(qc - NA_KW // 2, 0, GRID_W - NA_KW)
    col_in = (kc >= c0) & (kc < c0 + NA_KW)
    hit = [dc == t for t in range(N_DC)]
    neg = jnp.full((GRID_W, 2 * GRID_W), -jnp.inf, F32)
    b = []
    for d in range(N_DR):
        acc = jnp.zeros((GRID_W, 2 * GRID_W), F32)
        for t in range(N_DC):
            acc = jnp.where(hit[t], rpb_ref[h, d * N_DC + t], acc)
        b.append(jnp.where(col_in, acc, neg))
    for d in range(N_DR):
        if d + 1 < N_DR:
            o_ref[0, TAB_BOTH + d] = jnp.where(hi, b[d + 1], b[d])
        o_ref[0, TAB_LEFT + d] = jnp.where(hi, neg, b[d])
        o_ref[0, TAB_RIGHT + d] = jnp.where(hi, b[d], neg)


def _bias_table(rpb):
    rpb2 = rpb.reshape(NA_HEADS, N_DR * N_DC)
    return pl.pallas_call(
        _bias_kernel,
        grid=(NA_HEADS,),
        in_specs=[pl.BlockSpec(memory_space=pltpu.SMEM)],
        out_specs=pl.BlockSpec((1, N_TAB, GRID_W, 2 * GRID_W), lambda h: (h, 0, 0, 0)),
        out_shape=jax.ShapeDtypeStruct((NA_HEADS, N_TAB, GRID_W, 2 * GRID_W), F32),
        compiler_params=_params("arbitrary"),
        name="bias_table",
    )(rpb2)


_NT = (((1,), (1,)), ((), ()))


ATT_SB = 4
ATT_NSUB = 8
ATT_RB = ATT_SB * ATT_NSUB
ATT_WIN = ATT_SB + NA_KH
ATT_NBLK = ROWS // ATT_RB
LANES = 2 * GRID_W


def _band(first_row):
    win0 = min(max(first_row - NA_KH // 2, 0), ROWS - ATT_WIN)
    offs = tuple(min(max(first_row + a - NA_KH // 2, 0), ROWS - NA_KH) - win0 for a in range(ATT_SB))
    return offs, win0 - first_row + NA_KH - 1


def _band_tiles(a, off, c):
    p0, p1 = off // 2, (off + NA_KH - 1) // 2
    tiles = []
    for p in range(p0, p1 + 1):
        left = off <= 2 * p < off + NA_KH
        right = off <= 2 * p + 1 < off + NA_KH
        d = 2 * p - a + c
        tiles.append(TAB_BOTH + d if left and right else TAB_LEFT + d if left else TAB_RIGHT + d + 1)
    return p0, p1, tiles


def _attn_step(q_ref, k_ref, v_ref, kc_ref, vc_ref, tab_ref, o_ref, s_scr, sc_scr, p_scr, pc_scr, geoms):
    blk = pl.program_id(1)
    scale = HEAD_DIM ** -0.5
    nq = ATT_SB * GRID_W
    nk = ATT_WIN * GRID_W
    def window(t):
        first_row = (blk * ATT_NSUB + t) * ATT_SB
        w0 = jnp.clip(first_row - NA_KH // 2, 0, ROWS - ATT_WIN)
        return pl.multiple_of(w0 * GRID_W, GRID_W), slice(t * nq, (t + 1) * nq)

    def scores(t):
        start, qrows = window(t)
        q = q_ref[qrows, :]
        s_scr[t] = lax.dot_general(q, k_ref[pl.ds(start, nk), :], _NT, preferred_element_type=F32)
        sc_scr[t] = lax.dot_general(q, kc_ref[...], _NT, preferred_element_type=F32)

    def softmax(t):
        offs, c = geoms[t]
        denoms = []
        for a in range(ATT_SB):
            rows = slice(a * GRID_W, (a + 1) * GRID_W)
            p0, p1, tiles = _band_tiles(a, offs[a], c)
            lo, hi = p0 * LANES, (p1 + 1) * LANES
            bias = jnp.concatenate([tab_ref[0, i] for i in tiles], axis=1)
            s_loc = s_scr[t, rows, lo:hi] * scale + bias
            s_ctx = sc_scr[t, rows, :] * scale
            m = jnp.maximum(jnp.max(s_loc, axis=-1, keepdims=True), jnp.max(s_ctx, axis=-1, keepdims=True))
            p_loc = jnp.exp(s_loc - m)
            p_ctx = jnp.exp(s_ctx - m)
            denoms.append(jnp.sum(p_loc, axis=-1, keepdims=True) + jnp.sum(p_ctx, axis=-1, keepdims=True))
            if lo > 0:
                p_scr[t, rows, :lo] = jnp.zeros((GRID_W, lo), BF16)
            p_scr[t, rows, lo:hi] = p_loc.astype(BF16)
            if hi < nk:
                p_scr[t, rows, hi:] = jnp.zeros((GRID_W, nk - hi), BF16)
            pc_scr[t, rows, :] = p_ctx.astype(BF16)
        return jnp.concatenate(denoms, axis=0)

    def values(t, denom):
        start, qrows = window(t)
        o = jnp.dot(p_scr[t], v_ref[pl.ds(start, nk), :], preferred_element_type=F32)
        o = o + jnp.dot(pc_scr[t], vc_ref[...], preferred_element_type=F32)
        o_ref[qrows, :] = (o / denom).astype(o_ref.dtype)

    denoms = {}
    scores(0)
    for t in range(ATT_NSUB):
        if t + 1 < ATT_NSUB:
            scores(t + 1)
        denoms[t] = softmax(t)
        if t > 0:
            values(t - 1, denoms.pop(t - 1))
    values(ATT_NSUB - 1, denoms.pop(ATT_NSUB - 1))


def _attn_kernel(q_ref, k_ref, v_ref, kc_ref, vc_ref, tab_ref, side_src_ref, o_ref, side_dst_ref, *scratch):
    blk = pl.program_id(1)
    first, mid, last = _band(0), _band(ATT_SB), _band(ROWS - ATT_SB)
    _side_cast(side_src_ref, side_dst_ref)
    refs = (q_ref, k_ref, v_ref, kc_ref, vc_ref, tab_ref, o_ref) + scratch

    @pl.when(blk == 0)
    def _():
        _attn_step(*refs, (first,) + (mid,) * (ATT_NSUB - 1))

    @pl.when((blk > 0) & (blk < ATT_NBLK - 1))
    def _():
        _attn_step(*refs, (mid,) * ATT_NSUB)

    @pl.when(blk == ATT_NBLK - 1)
    def _():
        _attn_step(*refs, (mid,) * (ATT_NSUB - 1) + (last,))


def _attention(qkv, kvc, tab, side):
    n = qkv.shape[0]
    c = kvc.shape[0]
    nq = ATT_SB * GRID_W
    nk = ATT_WIN * GRID_W
    assert NA_HEADS * ATT_NBLK >= side.count
    s_in, s_out, s_shape = side.specs(ATT_NBLK)
    return pl.pallas_call(
        _attn_kernel,
        grid=(NA_HEADS, ATT_NBLK),
        in_specs=[
            pl.BlockSpec((ATT_NSUB * nq, HEAD_DIM), lambda h, b: (b, h)),
            pl.BlockSpec((n, HEAD_DIM), lambda h, b: (0, NA_HEADS + h)),
            pl.BlockSpec((n, HEAD_DIM), lambda h, b: (0, 2 * NA_HEADS + h)),
            pl.BlockSpec((c, HEAD_DIM), lambda h, b: (0, h)),
            pl.BlockSpec((c, HEAD_DIM), lambda h, b: (0, NA_HEADS + h)),
            pl.BlockSpec((1, N_TAB, GRID_W, LANES), lambda h, b: (h, 0, 0, 0)),
            s_in,
        ],
        out_specs=[pl.BlockSpec((ATT_NSUB * nq, HEAD_DIM), lambda h, b: (b, h)), s_out],
        out_shape=[jax.ShapeDtypeStruct((n, NA_WIDTH), BF16), s_shape],
        scratch_shapes=[
            pltpu.VMEM((ATT_NSUB, nq, nk), F32),
            pltpu.VMEM((ATT_NSUB, nq, c), F32),
            pltpu.VMEM((ATT_NSUB, nq, nk), BF16),
            pltpu.VMEM((ATT_NSUB, nq, c), BF16),
        ],
        compiler_params=_params("arbitrary", "arbitrary"),
        name="attention",
    )(qkv, qkv, qkv, kvc, kvc, tab, side.src)


def _outproj_kernel(ya_ref, yb_ref, w_ref, x_ref, gt_ref, o_ref):
    acc = jnp.dot(ya_ref[...], w_ref[:A_WIDTH, :], preferred_element_type=F32)
    acc = acc + jnp.dot(yb_ref[...], w_ref[A_WIDTH:, :], preferred_element_type=F32)
    o_ref[...] = x_ref[...] + gt_ref[...] * acc


def _outproj(ya, yb, w, x, gt, bm=1024, bn=512):
    m = ya.shape[0]
    k, n = w.shape
    return pl.pallas_call(
        _outproj_kernel,
        grid=(m // bm, n // bn),
        in_specs=[
            pl.BlockSpec((bm, A_WIDTH), lambda i, j: (i, 0)),
            pl.BlockSpec((bm, NA_WIDTH), lambda i, j: (i, 0)),
            pl.BlockSpec((k, bn), lambda i, j: (0, j)),
            pl.BlockSpec((bm, bn), lambda i, j: (i, j)),
            pl.BlockSpec((1, bn), lambda i, j: (0, j)),
        ],
        out_specs=pl.BlockSpec((bm, bn), lambda i, j: (i, j)),
        out_shape=jax.ShapeDtypeStruct((m, n), F32),
        compiler_params=_params("arbitrary", "arbitrary"),
        name="outproj",
    )(ya, yb, w, x, gt)


HALO = BF16_SUBLANES


def _upproj_kernel(h_ref, wa_ref, wg_ref, cw_ref, cb_ref, *rest, bm):
    if len(rest) == 4:
        _side_cast(rest[0], rest[2])
        o_ref, a_scr = rest[1], rest[3]
    else:
        o_ref, a_scr = rest
    a_scr[...] = jnp.dot(h_ref[...], wa_ref[...], preferred_element_type=F32)
    g = jnp.dot(h_ref[HALO:HALO + bm, :], wg_ref[...], preferred_element_type=F32)
    a_prev = a_scr[pl.ds(HALO - 1, bm), :]
    a_mid = a_scr[pl.ds(HALO, bm), :]
    a_next = a_scr[pl.ds(HALO + 1, bm), :]
    a = cb_ref[...] + ((a_prev * cw_ref[0:1, :] + a_mid * cw_ref[1:2, :]) + a_next * cw_ref[2:3, :])
    o_ref[...] = ((a * jax.nn.sigmoid(a)) * g).astype(o_ref.dtype)


def _upproj(h2p, wa, wg, cw, cb, m, pad_rows, col0, ncols, bn, bm=1024, side=None):
    k = h2p.shape[1]
    off = col0 // bn
    nj = ncols // bn
    wspec = pl.BlockSpec((k, bn), lambda i, j: (0, off + j))
    in_specs = [
        pl.BlockSpec((pl.Element(bm + 2 * HALO), pl.Element(k)),
                     lambda i, j: (pl.multiple_of(pad_rows - HALO + i * bm, HALO), 0)),
        wspec,
        wspec,
        pl.BlockSpec((3, bn), lambda i, j: (0, off + j)),
        pl.BlockSpec((1, bn), lambda i, j: (0, off + j)),
    ]
    out_specs = [pl.BlockSpec((bm, bn), lambda i, j: (i, j))]
    out_shape = [jax.ShapeDtypeStruct((m, ncols), BF16)]
    args = [h2p, wa, wg, cw, cb]
    if side is not None:
        assert (m // bm) * nj >= side.count
        s_in, s_out, s_shape = side.specs(nj)
        in_specs.append(s_in)
        out_specs.append(s_out)
        out_shape.append(s_shape)
        args.append(side.src)
    out = pl.pallas_call(
        functools.partial(_upproj_kernel, bm=bm),
        grid=(m // bm, nj),
        in_specs=in_specs,
        out_specs=out_specs,
        out_shape=out_shape,
        scratch_shapes=[pltpu.VMEM((bm + 2 * HALO, bn), F32)],
        compiler_params=_params("arbitrary", "arbitrary"),
        name="upproj",
    )(*args)
    return out if side is not None else out[0]


def _downproj_kernel(a0_ref, a1_ref, w_ref, x_ref, gt_ref, o_ref):
    k0 = a0_ref.shape[1]
    acc = jnp.dot(a0_ref[...], w_ref[:k0, :], preferred_element_type=F32)
    acc = acc + jnp.dot(a1_ref[...], w_ref[k0:, :], preferred_element_type=F32)
    o_ref[...] = x_ref[...] + gt_ref[...] * acc


def _downproj(a0, a1, w, x, gt, bm=512, bn=512):
    m, k0 = a0.shape
    k1 = a1.shape[1]
    k, n = w.shape
    assert k == k0 + k1
    return pl.pallas_call(
        _downproj_kernel,
        grid=(m // bm, n // bn),
        in_specs=[
            pl.BlockSpec((bm, k0), lambda i, j: (i, 0)),
            pl.BlockSpec((bm, k1), lambda i, j: (i, 0)),
            pl.BlockSpec((k, bn), lambda i, j: (0, j)),
            pl.BlockSpec((bm, bn), lambda i, j: (i, j)),
            pl.BlockSpec((1, bn), lambda i, j: (0, j)),
        ],
        out_specs=pl.BlockSpec((bm, bn), lambda i, j: (i, j)),
        out_shape=jax.ShapeDtypeStruct((m, n), F32),
        compiler_params=_params("arbitrary", "arbitrary"),
        name="downproj",
    )(a0, a1, w, x, gt)


def kernel(x, c, ctx, c_ctx, w_ada, b_ada, g_norm1, w_in, a_ln_g, a_ln_b, a_w_s, a_b_s, na_rpb, w_out,
           g_norm2, w_up, conv_w, conv_b, w_down, g_final):
    d = D_MODEL
    x2 = x[0]
    ctx2 = ctx[0]

    w_in_b = w_in[0].astype(BF16)
    w_s_b = a_w_s[0].astype(BF16)
    b_s = a_b_s[0].reshape(A_GROUPS, CHUNK, 1)

    cs = jnp.zeros((BF16_SUBLANES, d), F32).at[0].set(c[0]).at[1].set(c_ctx)
    mod = _adaln(cs, w_ada[0], b_ada[0].reshape(1, 6 * d))
    sh1, sc1, gt1, sh2, sc2, gt2 = [mod[0:1, t * d:(t + 1) * d] for t in range(6)]
    csh1, csc1 = mod[1:2, 0:d], mod[1:2, d:2 * d]

    g1 = g_norm1[0].reshape(1, d)
    h = _norm_mod(x2, g1, sc1, sh1)
    hc = _norm_mod(ctx2, g1, csc1, csh1)

    up_cols = 256
    cast_a = _SideCast(w_up[0], (d, up_cols), (0, 0), D_FF // up_cols, 1)
    cast_g = _SideCast(w_up[0], (d, up_cols), (0, D_FF // up_cols), D_FF // up_cols, 1)
    out_cols = 128
    cast_o = _SideCast(w_out[0], (d, out_cols), (0, 0), d // out_cols, 1)
    y_a, w_out_b = _gmlp(h, w_in_b, a_ln_g[0].reshape(1, A_WIDTH), a_ln_b[0].reshape(1, A_WIDTH), w_s_b, b_s, cast_o)
    qkv, w_up_a = _proj(h, w_in_b, 2 * A_WIDTH, 3 * NA_WIDTH, bm=1024, bn=1024, side=cast_a)
    kvc = _proj(hc, w_in_b, KV_START, 2 * NA_WIDTH, bm=CTX_LEN, bn=1024)
    tab = _bias_table(na_rpb[0])
    y_b, w_up_g = _attention(qkv, kvc, tab, cast_g)

    x_new = _outproj(y_a, y_b, w_out_b, x2, gt1)
    norm_bm = 256
    h2p = _norm_mod(x_new, g_norm2[0].reshape(1, d), sc2, sh2, bm=norm_bm, pad=True)
    ff_main = (D_FF // 512) * 512
    down_rows = 128
    cast_d = _SideCast(w_down[0], (down_rows, d), (0, 0), D_FF // down_rows, 0)
    up = functools.partial(_upproj, h2p, w_up_a, w_up_g, conv_w[0], conv_b[0].reshape(1, D_FF), m=SEQ, pad_rows=norm_bm)
    hid0, w_down_b = up(col0=0, ncols=ff_main, bn=512, side=cast_d)
    hid1 = up(col0=ff_main, ncols=D_FF - ff_main, bn=D_FF - ff_main)
    x_fin = _downproj(hid0, hid1, w_down_b, x_new, gt2)
    out = _final_norm(x_fin, g_final.reshape(1, d))
    return out[None]
```

```python
import functools
import math
from typing import NamedTuple

import jax
import jax.numpy as jnp
from jax import lax
from jax.experimental import pallas as pl
from jax.experimental.pallas import tpu as pltpu

F32 = jnp.float32
BF16 = jnp.bfloat16

D_MODEL = 4096
SEQ = 8192
GRID_W = 64
ROWS = SEQ // GRID_W
CTX_LEN = 256
CHUNK = 128
HEAD_DIM = 128
A_WIDTH = 2048
A_GROUPS = A_WIDTH // HEAD_DIM
NA_WIDTH = 2048
NA_HEADS = NA_WIDTH // HEAD_DIM
NA_KH = 8
NA_KW = 16
IN_COLS = 2 * A_WIDTH + 3 * NA_WIDTH
KV_START = 2 * A_WIDTH + NA_WIDTH
D_FF = 11008
EPS = 1e-6

BF16_SUBLANES = 16
V7X_VMEM_BYTES = 64 * 1024 * 1024
VMEM_LIMIT = V7X_VMEM_BYTES - 4 * 1024 * 1024


def _params(*sem):
    return pltpu.CompilerParams(dimension_semantics=sem, vmem_limit_bytes=VMEM_LIMIT)


def _ada_kernel(c_ref, w_ref, b_ref, o_ref):
    c = c_ref[...]
    a = (c * jax.nn.sigmoid(c)).astype(BF16)
    w = w_ref[...].astype(BF16)
    o_ref[...] = jnp.dot(a, w, preferred_element_type=F32) + b_ref[...]


def _adaln(cs, w, b, bn=512):
    rows, d = cs.shape
    n = w.shape[1]
    return pl.pallas_call(
        _ada_kernel,
        grid=(n // bn,),
        in_specs=[
            pl.BlockSpec((rows, d), lambda j: (0, 0)),
            pl.BlockSpec((d, bn), lambda j: (0, j)),
            pl.BlockSpec((1, bn), lambda j: (0, j)),
        ],
        out_specs=pl.BlockSpec((rows, bn), lambda j: (0, j)),
        out_shape=jax.ShapeDtypeStruct((rows, n), F32),
        compiler_params=_params("arbitrary"),
        name="adaln",
    )(cs, w, b)


def _norm_mod_kernel(x_ref, g_ref, sc_ref, sh_ref, o_ref, *, pad, nblk):
    def body():
        x = x_ref[...]
        y = x * lax.rsqrt(jnp.mean(x * x, axis=-1, keepdims=True) + EPS)
        y = y * g_ref[...]
        o_ref[...] = (y * (1.0 + sc_ref[...]) + sh_ref[...]).astype(o_ref.dtype)

    if pad:
        i = pl.program_id(0)
        is_pad = (i == 0) | (i == nblk + 1)

        @pl.when(is_pad)
        def _():
            o_ref[...] = jnp.zeros(o_ref.shape, o_ref.dtype)

        pl.when(jnp.logical_not(is_pad))(body)
    else:
        body()


def _norm_mod(x, g, sc, sh, bm=256, pad=False):
    m, d = x.shape
    nblk = m // bm
    vec = pl.BlockSpec((1, d), lambda i: (0, 0))
    if pad:
        x_spec = pl.BlockSpec((bm, d), lambda i: (jnp.clip(i - 1, 0, nblk - 1), 0))
        steps, rows = nblk + 2, m + 2 * bm
    else:
        x_spec = pl.BlockSpec((bm, d), lambda i: (i, 0))
        steps, rows = nblk, m
    return pl.pallas_call(
        functools.partial(_norm_mod_kernel, pad=pad, nblk=nblk),
        grid=(steps,),
        in_specs=[x_spec, vec, vec, vec],
        out_specs=pl.BlockSpec((bm, d), lambda i: (i, 0)),
        out_shape=jax.ShapeDtypeStruct((rows, d), BF16),
        compiler_params=_params("arbitrary"),
        name="norm_mod",
    )(x, g, sc, sh)


def _norm_kernel(x_ref, g_ref, o_ref):
    x = x_ref[...]
    y = x * lax.rsqrt(jnp.mean(x * x, axis=-1, keepdims=True) + EPS)
    o_ref[...] = y * g_ref[...]


def _final_norm(x, g, bm=512):
    m, d = x.shape
    return pl.pallas_call(
        _norm_kernel,
        grid=(m // bm,),
        in_specs=[pl.BlockSpec((bm, d), lambda i: (i, 0)), pl.BlockSpec((1, d), lambda i: (0, 0))],
        out_specs=pl.BlockSpec((bm, d), lambda i: (i, 0)),
        out_shape=jax.ShapeDtypeStruct((m, d), F32),
        compiler_params=_params("arbitrary"),
        name="final_norm",
    )(x, g)


class _SideCast(NamedTuple):
    src: jax.Array
    block: tuple[int, int]
    first: tuple[int, int]
    count: int
    axis: int

    def specs(self, nj):
        def step(i, j):
            return jnp.minimum(i * nj + j, self.count - 1)

        def src_idx(i, j):
            s = step(i, j)
            return (self.first[0] + s, self.first[1]) if self.axis == 0 else (self.first[0], self.first[1] + s)

        def dst_idx(i, j):
            s = step(i, j)
            return (s, 0) if self.axis == 0 else (0, s)

        rows = self.block[0] * (self.count if self.axis == 0 else 1)
        cols = self.block[1] * (self.count if self.axis == 1 else 1)
        return (pl.BlockSpec(self.block, src_idx), pl.BlockSpec(self.block, dst_idx),
                jax.ShapeDtypeStruct((rows, cols), BF16))


def _side_cast(src_ref, dst_ref):
    dst_ref[...] = src_ref[...].astype(BF16)


def _proj_kernel(a_ref, w_ref, *rest):
    if len(rest) == 3:
        _side_cast(rest[0], rest[2])
    o_ref = rest[-2] if len(rest) == 3 else rest[0]
    o_ref[...] = jnp.dot(a_ref[...], w_ref[...], preferred_element_type=F32).astype(o_ref.dtype)


def _proj(a, w, col0, n, bm, bn, side=None):
    m, k = a.shape
    off = col0 // bn
    nj = n // bn
    in_specs = [
        pl.BlockSpec((bm, k), lambda i, j: (i, 0)),
        pl.BlockSpec((k, bn), lambda i, j: (0, j + off)),
    ]
    out_specs = [pl.BlockSpec((bm, bn), lambda i, j: (i, j))]
    out_shape = [jax.ShapeDtypeStruct((m, n), BF16)]
    args = [a, w]
    if side is not None:
        assert (m // bm) * nj >= side.count
        s_in, s_out, s_shape = side.specs(nj)
        in_specs.append(s_in)
        out_specs.append(s_out)
        out_shape.append(s_shape)
        args.append(side.src)
    out = pl.pallas_call(
        _proj_kernel,
        grid=(m // bm, nj),
        in_specs=in_specs,
        out_specs=out_specs,
        out_shape=out_shape,
        compiler_params=_params("arbitrary", "arbitrary"),
        name="proj",
    )(*args)
    return out if side is not None else out[0]


def _gelu(x):
    return x * (0.5 * (1.0 + jnp.tanh(math.sqrt(2.0 / math.pi) * (x + 0.044715 * (x * x * x)))))


def _gmlp_kernel(h_ref, w_ref, lng_ref, lnb_ref, ws_ref, bs_ref, side_src_ref, o_ref, side_dst_ref,
                 v_scr, s1_scr, s2_scr, *, nv, bn, bm):
    j = pl.program_id(1)
    slab = bn // GMLP_SPLIT

    @pl.when(j < nv)
    def _():
        _side_cast(side_src_ref, side_dst_ref)
        ps1 = ps2 = None
        for t in range(GMLP_SPLIT):
            gv = _gelu(jnp.dot(h_ref[...], w_ref[:, t * slab:(t + 1) * slab], preferred_element_type=F32))
            v_scr[:, pl.ds(pl.multiple_of(j * bn + t * slab, slab), slab)] = gv
            q1 = jnp.sum(gv, axis=-1, keepdims=True)
            q2 = jnp.sum(gv * gv, axis=-1, keepdims=True)
            ps1 = q1 if ps1 is None else ps1 + q1
            ps2 = q2 if ps2 is None else ps2 + q2
        first = j == 0
        s1_scr[...] = jnp.where(first, ps1, s1_scr[...] + ps1)
        s2_scr[...] = jnp.where(first, ps2, s2_scr[...] + ps2)

    @pl.when(j >= nv)
    def _():
        _side_cast(side_src_ref, side_dst_ref)
        mu = s1_scr[...] * (1.0 / A_WIDTH)
        var = s2_scr[...] * (1.0 / A_WIDTH) - mu * mu
        rstd = lax.rsqrt(var + EPS)
        nchunk = bm // CHUNK
        for t in range(GMLP_SPLIT):
            gu = _gelu(jnp.dot(h_ref[...], w_ref[:, t * slab:(t + 1) * slab], preferred_element_type=F32))
            for gl in range(slab // HEAD_DIM):
                g = (j - nv) * (bn // HEAD_DIM) + (t * slab) // HEAD_DIM + gl
                gcols = pl.ds(pl.multiple_of(g * HEAD_DIM, HEAD_DIM), HEAD_DIM)
                vn = (v_scr[:, gcols] - mu) * rstd * lng_ref[:, gcols] + lnb_ref[:, gcols]
                vn = vn.astype(BF16)
                vcat = jnp.concatenate([vn[c * CHUNK:(c + 1) * CHUNK, :] for c in range(nchunk)], axis=1)
                s = jnp.dot(ws_ref[g], vcat, preferred_element_type=F32) + bs_ref[g]
                cs = slice(gl * HEAD_DIM, (gl + 1) * HEAD_DIM)
                os = slice(t * slab + gl * HEAD_DIM, t * slab + (gl + 1) * HEAD_DIM)
                for c in range(nchunk):
                    rs = slice(c * CHUNK, (c + 1) * CHUNK)
                    o_ref[rs, os] = (gu[rs, cs] * s[:, c * HEAD_DIM:(c + 1) * HEAD_DIM]).astype(o_ref.dtype)


GMLP_SPLIT = 2


def _gmlp(h, w_in, ln_g, ln_b, w_s, b_s, side, bm=1024, bn=1024):
    m, k = h.shape
    nv = A_WIDTH // bn
    nj = 2 * nv
    assert (m // bm) * nj >= side.count
    s_in, s_out, s_shape = side.specs(nj)
    kern = functools.partial(_gmlp_kernel, nv=nv, bn=bn, bm=bm)
    return pl.pallas_call(
        kern,
        grid=(m // bm, nj),
        in_specs=[
            pl.BlockSpec((bm, k), lambda i, j: (i, 0)),
            pl.BlockSpec((k, bn), lambda i, j: (0, jnp.where(j < nv, j + nv, j - nv))),
            pl.BlockSpec((1, A_WIDTH), lambda i, j: (0, 0)),
            pl.BlockSpec((1, A_WIDTH), lambda i, j: (0, 0)),
            pl.BlockSpec((A_GROUPS, CHUNK, CHUNK), lambda i, j: (0, 0, 0)),
            pl.BlockSpec((A_GROUPS, CHUNK, 1), lambda i, j: (0, 0, 0)),
            s_in,
        ],
        out_specs=[pl.BlockSpec((bm, bn), lambda i, j: (i, jnp.maximum(j - nv, 0))), s_out],
        out_shape=[jax.ShapeDtypeStruct((m, A_WIDTH), BF16), s_shape],
        scratch_shapes=[
            pltpu.VMEM((bm, A_WIDTH), F32),
            pltpu.VMEM((bm, 1), F32),
            pltpu.VMEM((bm, 1), F32),
        ],
        compiler_params=_params("arbitrary", "arbitrary"),
        name="gmlp",
    )(h, w_in, ln_g, ln_b, w_s, b_s, side.src)


N_DR = 2 * NA_KH - 1
N_DC = 2 * NA_KW - 1
TAB_BOTH = 0
TAB_LEFT = N_DR - 1
TAB_RIGHT = 2 * N_DR - 1
N_TAB = 3 * N_DR - 1


def _bias_kernel(rpb_ref, o_ref):
    h = pl.program_id(0)
    qc = lax.broadcasted_iota(jnp.int32, (GRID_W, 2 * GRID_W), 0)
    lane = lax.broadcasted_iota(jnp.int32, (GRID_W, 2 * GRID_W), 1)
    kc = lane & (GRID_W - 1)
    hi = lane >= GRID_W
    dc = jnp.clip(kc - qc, -(NA_KW - 1), NA_KW - 1) + (NA_KW - 1)
    c0 = jnp.clip(qc - NA_KW // 2, 0, GRID_W - NA_KW)
    col_in = (kc >= c0) & (kc < c0 + NA_KW)
    hit = [dc == t for t in range(N_DC)]
    neg = jnp.full((GRID_W, 2 * GRID_W), -jnp.inf, F32)
    b = []
    for d in range(N_DR):
        acc = jnp.zeros((GRID_W, 2 * GRID_W), F32)
        for t in range(N_DC):
            acc = jnp.where(hit[t], rpb_ref[h, d * N_DC + t], acc)
        b.append(jnp.where(col_in, acc, neg))
    for d in range(N_DR):
        if d + 1 < N_DR:
            o_ref[0, TAB_BOTH + d] = jnp.where(hi, b[d + 1], b[d])
        o_ref[0, TAB_LEFT + d] = jnp.where(hi, neg, b[d])
        o_ref[0, TAB_RIGHT + d] = jnp.where(hi, b[d], neg)


def _bias_table(rpb):
    rpb2 = rpb.reshape(NA_HEADS, N_DR * N_DC)
    return pl.pallas_call(
        _bias_kernel,
        grid=(NA_HEADS,),
        in_specs=[pl.BlockSpec(memory_space=pltpu.SMEM)],
        out_specs=pl.BlockSpec((1, N_TAB, GRID_W, 2 * GRID_W), lambda h: (h, 0, 0, 0)),
        out_shape=jax.ShapeDtypeStruct((NA_HEADS, N_TAB, GRID_W, 2 * GRID_W), F32),
        compiler_params=_params("arbitrary"),
        name="bias_table",
    )(rpb2)


_NT = (((1,), (1,)), ((), ()))


ATT_SB = 4
ATT_NSUB = 8
ATT_RB = ATT_SB * ATT_NSUB
ATT_WIN = ATT_SB + NA_KH
ATT_NBLK = ROWS // ATT_RB
LANES = 2 * GRID_W


def _band(first_row):
    win0 = min(max(first_row - NA_KH // 2, 0), ROWS - ATT_WIN)
    offs = tuple(min(max(first_row + a - NA_KH // 2, 0), ROWS - NA_KH) - win0 for a in range(ATT_SB))
    return offs, win0 - first_row + NA_KH - 1


def _band_tiles(a, off, c):
    p0, p1 = off // 2, (off + NA_KH - 1) // 2
    tiles = []
    for p in range(p0, p1 + 1):
        left = off <= 2 * p < off + NA_KH
        right = off <= 2 * p + 1 < off + NA_KH
        d = 2 * p - a + c
        tiles.append(TAB_BOTH + d if left and right else TAB_LEFT + d if left else TAB_RIGHT + d + 1)
    return p0, p1, tiles


def _attn_step(q_ref, k_ref, v_ref, kc_ref, vc_ref, tab_ref, o_ref, s_scr, sc_scr, p_scr, pc_scr, geoms):
    blk = pl.program_id(1)
    scale = HEAD_DIM ** -0.5
    nq = ATT_SB * GRID_W
    nk = ATT_WIN * GRID_W
    def window(t):
        first_row = (blk * ATT_NSUB + t) * ATT_SB
        w0 = jnp.clip(first_row - NA_KH // 2, 0, ROWS - ATT_WIN)
        return pl.multiple_of(w0 * GRID_W, GRID_W), slice(t * nq, (t + 1) * nq)

    def scores(t):
        start, qrows = window(t)
        q = q_ref[qrows, :]
        s_scr[t] = lax.dot_general(q, k_ref[pl.ds(start, nk), :], _NT, preferred_element_type=F32)
        sc_scr[t] = lax.dot_general(q, kc_ref[...], _NT, preferred_element_type=F32)

    def softmax(t):
        offs, c = geoms[t]
        denoms = []
        for a in range(ATT_SB):
            rows = slice(a * GRID_W, (a + 1) * GRID_W)
            p0, p1, tiles = _band_tiles(a, offs[a], c)
            lo, hi = p0 * LANES, (p1 + 1) * LANES
            bias = jnp.concatenate([tab_ref[0, i] for i in tiles], axis=1)
            s_loc = s_scr[t, rows, lo:hi] * scale + bias
            s_ctx = sc_scr[t, rows, :] * scale
            m = jnp.maximum(jnp.max(s_loc, axis=-1, keepdims=True), jnp.max(s_ctx, axis=-1, keepdims=True))
            p_loc = jnp.exp(s_loc - m)
            p_ctx = jnp.exp(s_ctx - m)
            denoms.append(jnp.sum(p_loc, axis=-1, keepdims=True) + jnp.sum(p_ctx, axis=-1, keepdims=True))
            if lo > 0:
                p_scr[t, rows, :lo] = jnp.zeros((GRID_W, lo), BF16)
            p_scr[t, rows, lo:hi] = p_loc.astype(BF16)
            if hi < nk:
                p_scr[t, rows, hi:] = jnp.zeros((GRID_W, nk - hi), BF16)
            pc_scr[t, rows, :] = p_ctx.astype(BF16)
        return jnp.concatenate(denoms, axis=0)

    def values(t, denom):
        start, qrows = window(t)
        o = jnp.dot(p_scr[t], v_ref[pl.ds(start, nk), :], preferred_element_type=F32)
        o = o + jnp.dot(pc_scr[t], vc_ref[...], preferred_element_type=F32)
        o_ref[qrows, :] = (o / denom).astype(o_ref.dtype)

    denoms = {}
    scores(0)
    for t in range(ATT_NSUB):
        if t + 1 < ATT_NSUB:
            scores(t + 1)
        denoms[t] = softmax(t)
        if t > 0:
            values(t - 1, denoms.pop(t - 1))
    values(ATT_NSUB - 1, denoms.pop(ATT_NSUB - 1))


def _attn_kernel(q_ref, k_ref, v_ref, kc_ref, vc_ref, tab_ref, side_src_ref, o_ref, side_dst_ref, *scratch):
    blk = pl.program_id(1)
    first, mid, last = _band(0), _band(ATT_SB), _band(ROWS - ATT_SB)
    refs = (q_ref, k_ref, v_ref, kc_ref, vc_ref, tab_ref, o_ref) + scratch

    def step(geoms):
        _side_cast(side_src_ref, side_dst_ref)
        _attn_step(*refs, geoms)

    @pl.when(blk == 0)
    def _():
        step((first,) + (mid,) * (ATT_NSUB - 1))

    @pl.when((blk > 0) & (blk < ATT_NBLK - 1))
    def _():
        step((mid,) * ATT_NSUB)

    @pl.when(blk == ATT_NBLK - 1)
    def _():
        step((mid,) * (ATT_NSUB - 1) + (last,))


def _attention(qkv, kvc, tab, side):
    n = qkv.shape[0]
    c = kvc.shape[0]
    nq = ATT_SB * GRID_W
    nk = ATT_WIN * GRID_W
    assert NA_HEADS * ATT_NBLK >= side.count
    s_in, s_out, s_shape = side.specs(ATT_NBLK)
    return pl.pallas_call(
        _attn_kernel,
        grid=(NA_HEADS, ATT_NBLK),
        in_specs=[
            pl.BlockSpec((ATT_NSUB * nq, HEAD_DIM), lambda h, b: (b, h)),
            pl.BlockSpec((n, HEAD_DIM), lambda h, b: (0, NA_HEADS + h)),
            pl.BlockSpec((n, HEAD_DIM), lambda h, b: (0, 2 * NA_HEADS + h)),
            pl.BlockSpec((c, HEAD_DIM), lambda h, b: (0, h)),
            pl.BlockSpec((c, HEAD_DIM), lambda h, b: (0, NA_HEADS + h)),
            pl.BlockSpec((1, N_TAB, GRID_W, LANES), lambda h, b: (h, 0, 0, 0)),
            s_in,
        ],
        out_specs=[pl.BlockSpec((ATT_NSUB * nq, HEAD_DIM), lambda h, b: (b, h)), s_out],
        out_shape=[jax.ShapeDtypeStruct((n, NA_WIDTH), BF16), s_shape],
        scratch_shapes=[
            pltpu.VMEM((ATT_NSUB, nq, nk), F32),
            pltpu.VMEM((ATT_NSUB, nq, c), F32),
            pltpu.VMEM((ATT_NSUB, nq, nk), BF16),
            pltpu.VMEM((ATT_NSUB, nq, c), BF16),
        ],
        compiler_params=_params("arbitrary", "arbitrary"),
        name="attention",
    )(qkv, qkv, qkv, kvc, kvc, tab, side.src)


def _outproj_kernel(ya_ref, yb_ref, w_ref, x_ref, gt_ref, o_ref):
    acc = jnp.dot(ya_ref[...], w_ref[:A_WIDTH, :], preferred_element_type=F32)
    acc = acc + jnp.dot(yb_ref[...], w_ref[A_WIDTH:, :], preferred_element_type=F32)
    o_ref[...] = x_ref[...] + gt_ref[...] * acc


def _outproj(ya, yb, w, x, gt, bm=1024, bn=1024):
    m = ya.shape[0]
    k, n = w.shape
    return pl.pallas_call(
        _outproj_kernel,
        grid=(m // bm, n // bn),
        in_specs=[
            pl.BlockSpec((bm, A_WIDTH), lambda i, j: (i, 0)),
            pl.BlockSpec((bm, NA_WIDTH), lambda i, j: (i, 0)),
            pl.BlockSpec((k, bn), lambda i, j: (0, j)),
            pl.BlockSpec((bm, bn), lambda i, j: (i, j)),
            pl.BlockSpec((1, bn), lambda i, j: (0, j)),
        ],
        out_specs=pl.BlockSpec((bm, bn), lambda i, j: (i, j)),
        out_shape=jax.ShapeDtypeStruct((m, n), F32),
        compiler_params=_params("arbitrary", "arbitrary"),
        name="outproj",
    )(ya, yb, w, x, gt)


HALO = BF16_SUBLANES


def _upproj_kernel(h_ref, wa_ref, wg_ref, cw_ref, cb_ref, *rest, bm):
    if len(rest) == 4:
        _side_cast(rest[0], rest[2])
        o_ref, a_scr = rest[1], rest[3]
    else:
        o_ref, a_scr = rest
    a_scr[...] = jnp.dot(h_ref[...], wa_ref[...], preferred_element_type=F32)
    g = jnp.dot(h_ref[HALO:HALO + bm, :], wg_ref[...], preferred_element_type=F32)
    a_prev = a_scr[pl.ds(HALO - 1, bm), :]
    a_mid = a_scr[pl.ds(HALO, bm), :]
    a_next = a_scr[pl.ds(HALO + 1, bm), :]
    a = cb_ref[...] + ((a_prev * cw_ref[0:1, :] + a_mid * cw_ref[1:2, :]) + a_next * cw_ref[2:3, :])
    o_ref[...] = ((a * jax.nn.sigmoid(a)) * g).astype(o_ref.dtype)


def _upproj(h2p, wa, wg, cw, cb, m, pad_rows, col0, ncols, bn, bm=1024, side=None):
    k = h2p.shape[1]
    off = col0 // bn
    nj = ncols // bn
    wspec = pl.BlockSpec((k, bn), lambda i, j: (0, off + j))
    in_specs = [
        pl.BlockSpec((pl.Element(bm + 2 * HALO), pl.Element(k)),
                     lambda i, j: (pl.multiple_of(pad_rows - HALO + i * bm, HALO), 0)),
        wspec,
        wspec,
        pl.BlockSpec((3, bn), lambda i, j: (0, off + j)),
        pl.BlockSpec((1, bn), lambda i, j: (0, off + j)),
    ]
    out_specs = [pl.BlockSpec((bm, bn), lambda i, j: (i, j))]
    out_shape = [jax.ShapeDtypeStruct((m, ncols), BF16)]
    args = [h2p, wa, wg, cw, cb]
    if side is not None:
        assert (m // bm) * nj >= side.count
        s_in, s_out, s_shape = side.specs(nj)
        in_specs.append(s_in)
        out_specs.append(s_out)
        out_shape.append(s_shape)
        args.append(side.src)
    out = pl.pallas_call(
        functools.partial(_upproj_kernel, bm=bm),
        grid=(m // bm, nj),
        in_specs=in_specs,
        out_specs=out_specs,
        out_shape=out_shape,
        scratch_shapes=[pltpu.VMEM((bm + 2 * HALO, bn), F32)],
        compiler_params=_params("arbitrary", "arbitrary"),
        name="upproj",
    )(*args)
    return out if side is not None else out[0]


def _downproj_kernel(a0_ref, a1_ref, w_ref, x_ref, gt_ref, o_ref):
    k0 = a0_ref.shape[1]
    acc = jnp.dot(a0_ref[...], w_ref[:k0, :], preferred_element_type=F32)
    acc = acc + jnp.dot(a1_ref[...], w_ref[k0:, :], preferred_element_type=F32)
    o_ref[...] = x_ref[...] + gt_ref[...] * acc


def _downproj(a0, a1, w, x, gt, bm=512, bn=512):
    m, k0 = a0.shape
    k1 = a1.shape[1]
    k, n = w.shape
    assert k == k0 + k1
    return pl.pallas_call(
        _downproj_kernel,
        grid=(n // bn, m // bm),
        in_specs=[
            pl.BlockSpec((bm, k0), lambda j, i: (i, 0)),
            pl.BlockSpec((bm, k1), lambda j, i: (i, 0)),
            pl.BlockSpec((k, bn), lambda j, i: (0, j)),
            pl.BlockSpec((bm, bn), lambda j, i: (i, j)),
            pl.BlockSpec((1, bn), lambda j, i: (0, j)),
        ],
        out_specs=pl.BlockSpec((bm, bn), lambda j, i: (i, j)),
        out_shape=jax.ShapeDtypeStruct((m, n), F32),
        compiler_params=_params("arbitrary", "arbitrary"),
        name="downproj",
    )(a0, a1, w, x, gt)


def kernel(x, c, ctx, c_ctx, w_ada, b_ada, g_norm1, w_in, a_ln_g, a_ln_b, a_w_s, a_b_s, na_rpb, w_out,
           g_norm2, w_up, conv_w, conv_b, w_down, g_final):
    d = D_MODEL
    x2 = x[0]
    ctx2 = ctx[0]

    w_in_b = w_in[0].astype(BF16)
    w_s_b = a_w_s[0].astype(BF16)
    b_s = a_b_s[0].reshape(A_GROUPS, CHUNK, 1)

    cs = jnp.zeros((BF16_SUBLANES, d), F32).at[0].set(c[0]).at[1].set(c_ctx)
    mod = _adaln(cs, w_ada[0], b_ada[0].reshape(1, 6 * d))
    sh1, sc1, gt1, sh2, sc2, gt2 = [mod[0:1, t * d:(t + 1) * d] for t in range(6)]
    csh1, csc1 = mod[1:2, 0:d], mod[1:2, d:2 * d]

    g1 = g_norm1[0].reshape(1, d)
    norm_bm = 256
    h = _norm_mod(x2, g1, sc1, sh1, bm=norm_bm)
    hc = _norm_mod(ctx2, g1, csc1, csh1, bm=CTX_LEN)

    up_cols = 256
    cast_a = _SideCast(w_up[0], (d, up_cols), (0, 0), D_FF // up_cols, 1)
    cast_g = _SideCast(w_up[0], (d, up_cols), (0, D_FF // up_cols), D_FF // up_cols, 1)
    out_cols = 128
    cast_o = _SideCast(w_out[0], (d, out_cols), (0, 0), d // out_cols, 1)
    y_a, w_out_b = _gmlp(h, w_in_b, a_ln_g[0].reshape(1, A_WIDTH), a_ln_b[0].reshape(1, A_WIDTH), w_s_b, b_s, cast_o)
    qkv, w_up_a = _proj(h, w_in_b, 2 * A_WIDTH, 3 * NA_WIDTH, bm=1024, bn=1024, side=cast_a)
    kvc = _proj(hc, w_in_b, KV_START, 2 * NA_WIDTH, bm=CTX_LEN, bn=1024)
    tab = _bias_table(na_rpb[0])
    y_b, w_up_g = _attention(qkv, kvc, tab, cast_g)

    x_new = _outproj(y_a, y_b, w_out_b, x2, gt1)
    h2p = _norm_mod(x_new, g_norm2[0].reshape(1, d), sc2, sh2, bm=norm_bm, pad=True)
    ff_main = (D_FF // 512) * 512
    down_rows = 128
    cast_d = _SideCast(w_down[0], (down_rows, d), (0, 0), D_FF // down_rows, 0)
    up = functools.partial(_upproj, h2p, w_up_a, w_up_g, conv_w[0], conv_b[0].reshape(1, D_FF), m=SEQ, pad_rows=norm_bm)
    hid0, w_down_b = up(col0=0, ncols=ff_main, bn=512, side=cast_d)
    hid1 = up(col0=ff_main, ncols=D_FF - ff_main, bn=D_FF - ff_main)
    x_fin = _downproj(hid0, hid1, w_down_b, x_new, gt2)
    out = _final_norm(x_fin, g_final.reshape(1, d))
    return out[None]
```

```python
import functools
import math
from typing import NamedTuple

import jax
import jax.numpy as jnp
from jax import lax
from jax.experimental import pallas as pl
from jax.experimental.pallas import tpu as pltpu

F32 = jnp.float32
BF16 = jnp.bfloat16

D_MODEL = 4096
SEQ = 8192
GRID_W = 64
ROWS = SEQ // GRID_W
CTX_LEN = 256
CHUNK = 128
HEAD_DIM = 128
A_WIDTH = 2048
A_GROUPS = A_WIDTH // HEAD_DIM
NA_WIDTH = 2048
NA_HEADS = NA_WIDTH // HEAD_DIM
NA_KH = 8
NA_KW = 16
IN_COLS = 2 * A_WIDTH + 3 * NA_WIDTH
KV_START = 2 * A_WIDTH + NA_WIDTH
D_FF = 11008
EPS = 1e-6

BF16_SUBLANES = 16
V7X_VMEM_BYTES = 64 * 1024 * 1024
VMEM_LIMIT = V7X_VMEM_BYTES - 4 * 1024 * 1024


def _params(*sem):
    return pltpu.CompilerParams(dimension_semantics=sem, vmem_limit_bytes=VMEM_LIMIT)


def _ada_kernel(c_ref, w_ref, b_ref, o_ref):
    c = c_ref[...]
    a = (c * jax.nn.sigmoid(c)).astype(BF16)
    w = w_ref[...].astype(BF16)
    o_ref[...] = jnp.dot(a, w, preferred_element_type=F32) + b_ref[...]


def _adaln(cs, w, b, bn=512):
    rows, d = cs.shape
    n = w.shape[1]
    return pl.pallas_call(
        _ada_kernel,
        grid=(n // bn,),
        in_specs=[
            pl.BlockSpec((rows, d), lambda j: (0, 0)),
            pl.BlockSpec((d, bn), lambda j: (0, j)),
            pl.BlockSpec((1, bn), lambda j: (0, j)),
        ],
        out_specs=pl.BlockSpec((rows, bn), lambda j: (0, j)),
        out_shape=jax.ShapeDtypeStruct((rows, n), F32),
        compiler_params=_params("arbitrary"),
        name="adaln",
    )(cs, w, b)


def _norm_mod_kernel(x_ref, g_ref, sc_ref, sh_ref, o_ref, *, pad, nblk):
    def body():
        x = x_ref[...]
        y = x * lax.rsqrt(jnp.mean(x * x, axis=-1, keepdims=True) + EPS)
        y = y * g_ref[...]
        o_ref[...] = (y * (1.0 + sc_ref[...]) + sh_ref[...]).astype(o_ref.dtype)

    if pad:
        i = pl.program_id(0)
        is_pad = (i == 0) | (i == nblk + 1)

        @pl.when(is_pad)
        def _():
            o_ref[...] = jnp.zeros(o_ref.shape, o_ref.dtype)

        pl.when(jnp.logical_not(is_pad))(body)
    else:
        body()


def _norm_mod(x, g, sc, sh, bm=256, pad=False):
    m, d = x.shape
    nblk = m // bm
    vec = pl.BlockSpec((1, d), lambda i: (0, 0))
    if pad:
        x_spec = pl.BlockSpec((bm, d), lambda i: (jnp.clip(i - 1, 0, nblk - 1), 0))
        steps, rows = nblk + 2, m + 2 * bm
    else:
        x_spec = pl.BlockSpec((bm, d), lambda i: (i, 0))
        steps, rows = nblk, m
    return pl.pallas_call(
        functools.partial(_norm_mod_kernel, pad=pad, nblk=nblk),
        grid=(steps,),
        in_specs=[x_spec, vec, vec, vec],
        out_specs=pl.BlockSpec((bm, d), lambda i: (i, 0)),
        out_shape=jax.ShapeDtypeStruct((rows, d), BF16),
        compiler_params=_params("arbitrary"),
        name="norm_mod",
    )(x, g, sc, sh)


def _norm_kernel(x_ref, g_ref, o_ref):
    x = x_ref[...]
    y = x * lax.rsqrt(jnp.mean(x * x, axis=-1, keepdims=True) + EPS)
    o_ref[...] = y * g_ref[...]


def _final_norm(x, g, bm=512):
    m, d = x.shape
    return pl.pallas_call(
        _norm_kernel,
        grid=(m // bm,),
        in_specs=[pl.BlockSpec((bm, d), lambda i: (i, 0)), pl.BlockSpec((1, d), lambda i: (0, 0))],
        out_specs=pl.BlockSpec((bm, d), lambda i: (i, 0)),
        out_shape=jax.ShapeDtypeStruct((m, d), F32),
        compiler_params=_params("arbitrary"),
        name="final_norm",
    )(x, g)


class _SideCast(NamedTuple):
    src: jax.Array
    block: tuple[int, int]
    first: tuple[int, int]
    count: int
    axis: int

    def specs(self, nj):
        def step(i, j):
            return jnp.minimum(i * nj + j, self.count - 1)

        def src_idx(i, j):
            s = step(i, j)
            return (self.first[0] + s, self.first[1]) if self.axis == 0 else (self.first[0], self.first[1] + s)

        def dst_idx(i, j):
            s = step(i, j)
            return (s, 0) if self.axis == 0 else (0, s)

        rows = self.block[0] * (self.count if self.axis == 0 else 1)
        cols = self.block[1] * (self.count if self.axis == 1 else 1)
        return (pl.BlockSpec(self.block, src_idx), pl.BlockSpec(self.block, dst_idx),
                jax.ShapeDtypeStruct((rows, cols), BF16))


def _side_cast(src_ref, dst_ref):
    dst_ref[...] = src_ref[...].astype(BF16)


def _proj_kernel(a_ref, w_ref, *rest):
    if len(rest) == 3:
        _side_cast(rest[0], rest[2])
    o_ref = rest[-2] if len(rest) == 3 else rest[0]
    o_ref[...] = jnp.dot(a_ref[...], w_ref[...], preferred_element_type=F32).astype(o_ref.dtype)


def _proj(a, w, col0, n, bm, bn, side=None):
    m, k = a.shape
    off = col0 // bn
    nj = n // bn
    in_specs = [
        pl.BlockSpec((bm, k), lambda i, j: (i, 0)),
        pl.BlockSpec((k, bn), lambda i, j: (0, j + off)),
    ]
    out_specs = [pl.BlockSpec((bm, bn), lambda i, j: (i, j))]
    out_shape = [jax.ShapeDtypeStruct((m, n), BF16)]
    args = [a, w]
    if side is not None:
        assert (m // bm) * nj >= side.count
        s_in, s_out, s_shape = side.specs(nj)
        in_specs.append(s_in)
        out_specs.append(s_out)
        out_shape.append(s_shape)
        args.append(side.src)
    out = pl.pallas_call(
        _proj_kernel,
        grid=(m // bm, nj),
        in_specs=in_specs,
        out_specs=out_specs,
        out_shape=out_shape,
        compiler_params=_params("arbitrary", "arbitrary"),
        name="proj",
    )(*args)
    return out if side is not None else out[0]


def _gelu(x):
    return x * (0.5 * (1.0 + jnp.tanh(math.sqrt(2.0 / math.pi) * (x + 0.044715 * (x * x * x)))))


def _gmlp_kernel(h_ref, w_ref, lng_ref, lnb_ref, ws_ref, bs_ref, side_src_ref, o_ref, side_dst_ref,
                 v_scr, s1_scr, s2_scr, *, nv, bn, bm):
    j = pl.program_id(1)
    slab = bn // GMLP_SPLIT

    @pl.when(j < nv)
    def _():
        _side_cast(side_src_ref, side_dst_ref)
        ps1 = ps2 = None
        for t in range(GMLP_SPLIT):
            gv = _gelu(jnp.dot(h_ref[...], w_ref[:, t * slab:(t + 1) * slab], preferred_element_type=F32))
            v_scr[:, pl.ds(pl.multiple_of(j * bn + t * slab, slab), slab)] = gv
            q1 = jnp.sum(gv, axis=-1, keepdims=True)
            q2 = jnp.sum(gv * gv, axis=-1, keepdims=True)
            ps1 = q1 if ps1 is None else ps1 + q1
            ps2 = q2 if ps2 is None else ps2 + q2
        first = j == 0
        s1_scr[...] = jnp.where(first, ps1, s1_scr[...] + ps1)
        s2_scr[...] = jnp.where(first, ps2, s2_scr[...] + ps2)

    @pl.when(j >= nv)
    def _():
        _side_cast(side_src_ref, side_dst_ref)
        mu = s1_scr[...] * (1.0 / A_WIDTH)
        var = s2_scr[...] * (1.0 / A_WIDTH) - mu * mu
        rstd = lax.rsqrt(var + EPS)
        nchunk = bm // CHUNK
        for t in range(GMLP_SPLIT):
            gu = _gelu(jnp.dot(h_ref[...], w_ref[:, t * slab:(t + 1) * slab], preferred_element_type=F32))
            for gl in range(slab // HEAD_DIM):
                g = (j - nv) * (bn // HEAD_DIM) + (t * slab) // HEAD_DIM + gl
                gcols = pl.ds(pl.multiple_of(g * HEAD_DIM, HEAD_DIM), HEAD_DIM)
                vn = (v_scr[:, gcols] - mu) * rstd * lng_ref[:, gcols] + lnb_ref[:, gcols]
                vn = vn.astype(BF16)
                vcat = jnp.concatenate([vn[c * CHUNK:(c + 1) * CHUNK, :] for c in range(nchunk)], axis=1)
                s = jnp.dot(ws_ref[g], vcat, preferred_element_type=F32) + bs_ref[g]
                cs = slice(gl * HEAD_DIM, (gl + 1) * HEAD_DIM)
                os = slice(t * slab + gl * HEAD_DIM, t * slab + (gl + 1) * HEAD_DIM)
                for c in range(nchunk):
                    rs = slice(c * CHUNK, (c + 1) * CHUNK)
                    o_ref[rs, os] = (gu[rs, cs] * s[:, c * HEAD_DIM:(c + 1) * HEAD_DIM]).astype(o_ref.dtype)


GMLP_SPLIT = 2


def _gmlp(h, w_in, ln_g, ln_b, w_s, b_s, side, bm=1024, bn=1024):
    m, k = h.shape
    nv = A_WIDTH // bn
    nj = 2 * nv
    assert (m // bm) * nj >= side.count
    s_in, s_out, s_shape = side.specs(nj)
    kern = functools.partial(_gmlp_kernel, nv=nv, bn=bn, bm=bm)
    return pl.pallas_call(
        kern,
        grid=(m // bm, nj),
        in_specs=[
            pl.BlockSpec((bm, k), lambda i, j: (i, 0)),
            pl.BlockSpec((k, bn), lambda i, j: (0, jnp.where(j < nv, j + nv, j - nv))),
            pl.BlockSpec((1, A_WIDTH), lambda i, j: (0, 0)),
            pl.BlockSpec((1, A_WIDTH), lambda i, j: (0, 0)),
            pl.BlockSpec((A_GROUPS, CHUNK, CHUNK), lambda i, j: (0, 0, 0)),
            pl.BlockSpec((A_GROUPS, CHUNK, 1), lambda i, j: (0, 0, 0)),
            s_in,
        ],
        out_specs=[pl.BlockSpec((bm, bn), lambda i, j: (i, jnp.maximum(j - nv, 0))), s_out],
        out_shape=[jax.ShapeDtypeStruct((m, A_WIDTH), BF16), s_shape],
        scratch_shapes=[
            pltpu.VMEM((bm, A_WIDTH), F32),
            pltpu.VMEM((bm, 1), F32),
            pltpu.VMEM((bm, 1), F32),
        ],
        compiler_params=_params("arbitrary", "arbitrary"),
        name="gmlp",
    )(h, w_in, ln_g, ln_b, w_s, b_s, side.src)


N_DR = 2 * NA_KH - 1
N_DC = 2 * NA_KW - 1
TAB_BOTH = 0
TAB_LEFT = N_DR - 1
TAB_RIGHT = 2 * N_DR - 1
N_TAB = 3 * N_DR - 1


def _bias_kernel(rpb_ref, o_ref):
    h = pl.program_id(0)
    qc = lax.broadcasted_iota(jnp.int32, (GRID_W, 2 * GRID_W), 0)
    lane = lax.broadcasted_iota(jnp.int32, (GRID_W, 2 * GRID_W), 1)
    kc = lane & (GRID_W - 1)
    hi = lane >= GRID_W
    dc = jnp.clip(kc - qc, -(NA_KW - 1), NA_KW - 1) + (NA_KW - 1)
    c0 = jnp.clip(qc - NA_KW // 2, 0, GRID_W - NA_KW)
    col_in = (kc >= c0) & (kc < c0 + NA_KW)
    hit = [dc == t for t in range(N_DC)]
    neg = jnp.full((GRID_W, 2 * GRID_W), -jnp.inf, F32)
    b = []
    for d in range(N_DR):
        acc = jnp.zeros((GRID_W, 2 * GRID_W), F32)
        for t in range(N_DC):
            acc = jnp.where(hit[t], rpb_ref[h, d * N_DC + t], acc)
        b.append(jnp.where(col_in, acc, neg))
    for d in range(N_DR):
        if d + 1 < N_DR:
            o_ref[0, TAB_BOTH + d] = jnp.where(hi, b[d + 1], b[d])
        o_ref[0, TAB_LEFT + d] = jnp.where(hi, neg, b[d])
        o_ref[0, TAB_RIGHT + d] = jnp.where(hi, b[d], neg)


def _bias_table(rpb):
    rpb2 = rpb.reshape(NA_HEADS, N_DR * N_DC)
    return pl.pallas_call(
        _bias_kernel,
        grid=(NA_HEADS,),
        in_specs=[pl.BlockSpec(memory_space=pltpu.SMEM)],
        out_specs=pl.BlockSpec((1, N_TAB, GRID_W, 2 * GRID_W), lambda h: (h, 0, 0, 0)),
        out_shape=jax.ShapeDtypeStruct((NA_HEADS, N_TAB, GRID_W, 2 * GRID_W), F32),
        compiler_params=_params("arbitrary"),
        name="bias_table",
    )(rpb2)


_NT = (((1,), (1,)), ((), ()))


ATT_SB = 4
ATT_NSUB = 8
ATT_RB = ATT_SB * ATT_NSUB
ATT_WIN = ATT_SB + NA_KH
ATT_NBLK = ROWS // ATT_RB
LANES = 2 * GRID_W


def _band(first_row):
    win0 = min(max(first_row - NA_KH // 2, 0), ROWS - ATT_WIN)
    offs = tuple(min(max(first_row + a - NA_KH // 2, 0), ROWS - NA_KH) - win0 for a in range(ATT_SB))
    return offs, win0 - first_row + NA_KH - 1


def _band_tiles(a, off, c):
    p0, p1 = off // 2, (off + NA_KH - 1) // 2
    tiles = []
    for p in range(p0, p1 + 1):
        left = off <= 2 * p < off + NA_KH
        right = off <= 2 * p + 1 < off + NA_KH
        d = 2 * p - a + c
        tiles.append(TAB_BOTH + d if left and right else TAB_LEFT + d if left else TAB_RIGHT + d + 1)
    return p0, p1, tiles


def _attn_step(q_ref, k_ref, v_ref, kc_ref, vc_ref, tab_ref, o_ref, s_scr, sc_scr, p_scr, pc_scr, geoms):
    blk = pl.program_id(1)
    scale = HEAD_DIM ** -0.5
    nq = ATT_SB * GRID_W
    nk = ATT_WIN * GRID_W
    def window(t):
        first_row = (blk * ATT_NSUB + t) * ATT_SB
        w0 = jnp.clip(first_row - NA_KH // 2, 0, ROWS - ATT_WIN)
        return pl.multiple_of(w0 * GRID_W, GRID_W), slice(t * nq, (t + 1) * nq)

    def scores(t):
        start, qrows = window(t)
        q = q_ref[qrows, :]
        s_scr[t] = lax.dot_general(q, k_ref[pl.ds(start, nk), :], _NT, preferred_element_type=F32)
        sc_scr[t] = lax.dot_general(q, kc_ref[...], _NT, preferred_element_type=F32)

    def softmax(t):
        offs, c = geoms[t]
        denoms = []
        for a in range(ATT_SB):
            rows = slice(a * GRID_W, (a + 1) * GRID_W)
            p0, p1, tiles = _band_tiles(a, offs[a], c)
            lo, hi = p0 * LANES, (p1 + 1) * LANES
            bias = jnp.concatenate([tab_ref[0, i] for i in tiles], axis=1)
            s_loc = s_scr[t, rows, lo:hi] * scale + bias
            s_ctx = sc_scr[t, rows, :] * scale
            m = jnp.maximum(jnp.max(s_loc, axis=-1, keepdims=True), jnp.max(s_ctx, axis=-1, keepdims=True))
            p_loc = jnp.exp(s_loc - m)
            p_ctx = jnp.exp(s_ctx - m)
            denoms.append(jnp.sum(p_loc, axis=-1, keepdims=True) + jnp.sum(p_ctx, axis=-1, keepdims=True))
            if lo > 0:
                p_scr[t, rows, :lo] = jnp.zeros((GRID_W, lo), BF16)
            p_scr[t, rows, lo:hi] = p_loc.astype(BF16)
            if hi < nk:
                p_scr[t, rows, hi:] = jnp.zeros((GRID_W, nk - hi), BF16)
            pc_scr[t, rows, :] = p_ctx.astype(BF16)
        return jnp.concatenate(denoms, axis=0)

    def values(t, denom):
        start, qrows = window(t)
        o = jnp.dot(p_scr[t], v_ref[pl.ds(start, nk), :], preferred_element_type=F32)
        o = o + jnp.dot(pc_scr[t], vc_ref[...], preferred_element_type=F32)
        o_ref[qrows, :] = (o / denom).astype(o_ref.dtype)

    denoms = {}
    scores(0)
    for t in range(ATT_NSUB):
        if t + 1 < ATT_NSUB:
            scores(t + 1)
        denoms[t] = softmax(t)
        if t > 0:
            values(t - 1, denoms.pop(t - 1))
    values(ATT_NSUB - 1, denoms.pop(ATT_NSUB - 1))


def _attn_kernel(q_ref, k_ref, v_ref, kc_ref, vc_ref, tab_ref, side_src_ref, o_ref, side_dst_ref, *scratch):
    blk = pl.program_id(1)
    first, mid, last = _band(0), _band(ATT_SB), _band(ROWS - ATT_SB)
    refs = (q_ref, k_ref, v_ref, kc_ref, vc_ref, tab_ref, o_ref) + scratch

    def step(geoms):
        _side_cast(side_src_ref, side_dst_ref)
        _attn_step(*refs, geoms)

    @pl.when(blk == 0)
    def _():
        step((first,) + (mid,) * (ATT_NSUB - 1))

    @pl.when((blk > 0) & (blk < ATT_NBLK - 1))
    def _():
        step((mid,) * ATT_NSUB)

    @pl.when(blk == ATT_NBLK - 1)
    def _():
        step((mid,) * (ATT_NSUB - 1) + (last,))


def _attention(qkv, kvc, tab, side):
    n = qkv.shape[0]
    c = kvc.shape[0]
    nq = ATT_SB * GRID_W
    nk = ATT_WIN * GRID_W
    assert NA_HEADS * ATT_NBLK >= side.count
    s_in, s_out, s_shape = side.specs(ATT_NBLK)
    return pl.pallas_call(
        _attn_kernel,
        grid=(NA_HEADS, ATT_NBLK),
        in_specs=[
            pl.BlockSpec((ATT_NSUB * nq, HEAD_DIM), lambda h, b: (b, h)),
            pl.BlockSpec((n, HEAD_DIM), lambda h, b: (0, NA_HEADS + h)),
            pl.BlockSpec((n, HEAD_DIM), lambda h, b: (0, 2 * NA_HEADS + h)),
            pl.BlockSpec((c, HEAD_DIM), lambda h, b: (0, h)),
            pl.BlockSpec((c, HEAD_DIM), lambda h, b: (0, NA_HEADS + h)),
            pl.BlockSpec((1, N_TAB, GRID_W, LANES), lambda h, b: (h, 0, 0, 0)),
            s_in,
        ],
        out_specs=[pl.BlockSpec((ATT_NSUB * nq, HEAD_DIM), lambda h, b: (b, h)), s_out],
        out_shape=[jax.ShapeDtypeStruct((n, NA_WIDTH), BF16), s_shape],
        scratch_shapes=[
            pltpu.VMEM((ATT_NSUB, nq, nk), F32),
            pltpu.VMEM((ATT_NSUB, nq, c), F32),
            pltpu.VMEM((ATT_NSUB, nq, nk), BF16),
            pltpu.VMEM((ATT_NSUB, nq, c), BF16),
        ],
        compiler_params=_params("arbitrary", "arbitrary"),
        name="attention",
    )(qkv, qkv, qkv, kvc, kvc, tab, side.src)


def _outproj_kernel(ya_ref, yb_ref, w_ref, x_ref, gt_ref, o_ref):
    acc = jnp.dot(ya_ref[...], w_ref[:A_WIDTH, :], preferred_element_type=F32)
    acc = acc + jnp.dot(yb_ref[...], w_ref[A_WIDTH:, :], preferred_element_type=F32)
    o_ref[...] = x_ref[...] + gt_ref[...] * acc


def _outproj(ya, yb, w, x, gt, bm=1024, bn=1024):
    m = ya.shape[0]
    k, n = w.shape
    return pl.pallas_call(
        _outproj_kernel,
        grid=(m // bm, n // bn),
        in_specs=[
            pl.BlockSpec((bm, A_WIDTH), lambda i, j: (i, 0)),
            pl.BlockSpec((bm, NA_WIDTH), lambda i, j: (i, 0)),
            pl.BlockSpec((k, bn), lambda i, j: (0, j)),
            pl.BlockSpec((bm, bn), lambda i, j: (i, j)),
            pl.BlockSpec((1, bn), lambda i, j: (0, j)),
        ],
        out_specs=pl.BlockSpec((bm, bn), lambda i, j: (i, j)),
        out_shape=jax.ShapeDtypeStruct((m, n), F32),
        compiler_params=_params("arbitrary", "arbitrary"),
        name="outproj",
    )(ya, yb, w, x, gt)


HALO = BF16_SUBLANES


def _upproj_kernel(h_ref, wa_ref, wg_ref, cw_ref, cb_ref, *rest, bm):
    if len(rest) == 3:
        _side_cast(rest[0], rest[2])
        o_ref = rest[1]
    else:
        o_ref, = rest
    n_ext = bm + 2 * HALO
    a_ext = jnp.dot(h_ref[...], wa_ref[...], preferred_element_type=F32)
    g = jnp.dot(h_ref[HALO:HALO + bm, :], wg_ref[...], preferred_element_type=F32)
    a_prev = pltpu.roll(a_ext, 1, 0)[HALO:HALO + bm, :]
    a_mid = a_ext[HALO:HALO + bm, :]
    a_next = pltpu.roll(a_ext, n_ext - 1, 0)[HALO:HALO + bm, :]
    a = cb_ref[...] + ((a_prev * cw_ref[0:1, :] + a_mid * cw_ref[1:2, :]) + a_next * cw_ref[2:3, :])
    o_ref[...] = ((a * jax.nn.sigmoid(a)) * g).astype(o_ref.dtype)


def _upproj(h2p, wa, wg, cw, cb, m, pad_rows, col0, ncols, bn, bm=1024, side=None):
    k = h2p.shape[1]
    off = col0 // bn
    nj = ncols // bn
    wspec = pl.BlockSpec((k, bn), lambda i, j: (0, off + j))
    in_specs = [
        pl.BlockSpec((pl.Element(bm + 2 * HALO), pl.Element(k)),
                     lambda i, j: (pl.multiple_of(pad_rows - HALO + i * bm, HALO), 0)),
        wspec,
        wspec,
        pl.BlockSpec((3, bn), lambda i, j: (0, off + j)),
        pl.BlockSpec((1, bn), lambda i, j: (0, off + j)),
    ]
    out_specs = [pl.BlockSpec((bm, bn), lambda i, j: (i, j))]
    out_shape = [jax.ShapeDtypeStruct((m, ncols), BF16)]
    args = [h2p, wa, wg, cw, cb]
    if side is not None:
        assert (m // bm) * nj >= side.count
        s_in, s_out, s_shape = side.specs(nj)
        in_specs.append(s_in)
        out_specs.append(s_out)
        out_shape.append(s_shape)
        args.append(side.src)
    out = pl.pallas_call(
        functools.partial(_upproj_kernel, bm=bm),
        grid=(m // bm, nj),
        in_specs=in_specs,
        out_specs=out_specs,
        out_shape=out_shape,
        compiler_params=_params("arbitrary", "arbitrary"),
        name="upproj",
    )(*args)
    return out if side is not None else out[0]


def _downproj_kernel(a0_ref, a1_ref, w_ref, x_ref, gt_ref, o_ref):
    k0 = a0_ref.shape[1]
    acc = jnp.dot(a0_ref[...], w_ref[:k0, :], preferred_element_type=F32)
    acc = acc + jnp.dot(a1_ref[...], w_ref[k0:, :], preferred_element_type=F32)
    o_ref[...] = x_ref[...] + gt_ref[...] * acc


def _downproj(a0, a1, w, x, gt, bm=512, bn=512):
    m, k0 = a0.shape
    k1 = a1.shape[1]
    k, n = w.shape
    assert k == k0 + k1
    return pl.pallas_call(
        _downproj_kernel,
        grid=(n // bn, m // bm),
        in_specs=[
            pl.BlockSpec((bm, k0), lambda j, i: (i, 0)),
            pl.BlockSpec((bm, k1), lambda j, i: (i, 0)),
            pl.BlockSpec((k, bn), lambda j, i: (0, j)),
            pl.BlockSpec((bm, bn), lambda j, i: (i, j)),
            pl.BlockSpec((1, bn), lambda j, i: (0, j)),
        ],
        out_specs=pl.BlockSpec((bm, bn), lambda j, i: (i, j)),
        out_shape=jax.ShapeDtypeStruct((m, n), F32),
        compiler_params=_params("arbitrary", "arbitrary"),
        name="downproj",
    )(a0, a1, w, x, gt)


def kernel(x, c, ctx, c_ctx, w_ada, b_ada, g_norm1, w_in, a_ln_g, a_ln_b, a_w_s, a_b_s, na_rpb, w_out,
           g_norm2, w_up, conv_w, conv_b, w_down, g_final):
    d = D_MODEL
    x2 = x[0]
    ctx2 = ctx[0]

    w_in_b = w_in[0].astype(BF16)
    w_s_b = a_w_s[0].astype(BF16)
    b_s = a_b_s[0].reshape(A_GROUPS, CHUNK, 1)

    cs = jnp.zeros((BF16_SUBLANES, d), F32).at[0].set(c[0]).at[1].set(c_ctx)
    mod = _adaln(cs, w_ada[0], b_ada[0].reshape(1, 6 * d))
    sh1, sc1, gt1, sh2, sc2, gt2 = [mod[0:1, t * d:(t + 1) * d] for t in range(6)]
    csh1, csc1 = mod[1:2, 0:d], mod[1:2, d:2 * d]

    g1 = g_norm1[0].reshape(1, d)
    norm_bm = 256
    h = _norm_mod(x2, g1, sc1, sh1, bm=norm_bm)
    hc = _norm_mod(ctx2, g1, csc1, csh1, bm=CTX_LEN)

    up_cols = 256
    cast_a = _SideCast(w_up[0], (d, up_cols), (0, 0), D_FF // up_cols, 1)
    cast_g = _SideCast(w_up[0], (d, up_cols), (0, D_FF // up_cols), D_FF // up_cols, 1)
    out_cols = 128
    cast_o = _SideCast(w_out[0], (d, out_cols), (0, 0), d // out_cols, 1)
    y_a, w_out_b = _gmlp(h, w_in_b, a_ln_g[0].reshape(1, A_WIDTH), a_ln_b[0].reshape(1, A_WIDTH), w_s_b, b_s, cast_o)
    qkv, w_up_a = _proj(h, w_in_b, 2 * A_WIDTH, 3 * NA_WIDTH, bm=1024, bn=1024, side=cast_a)
    kvc = _proj(hc, w_in_b, KV_START, 2 * NA_WIDTH, bm=CTX_LEN, bn=1024)
    tab = _bias_table(na_rpb[0])
    y_b, w_up_g = _attention(qkv, kvc, tab, cast_g)

    x_new = _outproj(y_a, y_b, w_out_b, x2, gt1)
    h2p = _norm_mod(x_new, g_norm2[0].reshape(1, d), sc2, sh2, bm=norm_bm, pad=True)
    ff_main = (D_FF // 512) * 512
    down_rows = 128
    cast_d = _SideCast(w_down[0], (down_rows, d), (0, 0), D_FF // down_rows, 0)
    up = functools.partial(_upproj, h2p, w_up_a, w_up_g, conv_w[0], conv_b[0].reshape(1, D_FF), m=SEQ, pad_rows=norm_bm)
    hid0, w_down_b = up(col0=0, ncols=ff_main, bn=512, side=cast_d)
    hid1 = up(col0=ff_main, ncols=D_FF - ff_main, bn=D_FF - ff_main)
    x_fin = _downproj(hid0, hid1, w_down_b, x_new, gt2)
    out = _final_norm(x_fin, g_final.reshape(1, d))
    return out[None]
```
